```python
import math, functools
import jax, jax.numpy as jnp
from jax import lax
import numpy as np

D_MODEL = 1024
BATCH = 4
SEQ = 4096
DEPTH = 4
DEC_BATCH = 8
DEC_SEQ = 8192
PAST_LEN = 128

FNET_WIDTH = D_MODEL
FNET_GROUPS = 4
FNET_GROUP_DIM = FNET_WIDTH // FNET_GROUPS
RET_HEAD_QK = 256
RET_HEADS = D_MODEL // RET_HEAD_QK
RET_HEAD_V = 2 * RET_HEAD_QK
RET_QK = RET_HEADS * RET_HEAD_QK
RET_V = RET_HEADS * RET_HEAD_V
RET_CHUNK = 128
ROPE_BASE = 10000.0
CONV_WIDTH = D_MODEL
CONV_K = 3
FFN_HIDDEN = -(-8 * D_MODEL // (3 * 256)) * 256
DEEPNORM_ALPHA = (2.0 * DEPTH) ** 0.25
DEEPNORM_BETA = (8.0 * DEPTH) ** -0.25
LN_EPS = 1e-5
HEAD_NORM_EPS = 1e-6
IN_SPLITS = (FNET_WIDTH, RET_QK, RET_QK, RET_V, RET_V,
             CONV_WIDTH, CONV_WIDTH, CONV_WIDTH, D_MODEL, D_MODEL, D_MODEL)
IN_COLS = sum(IN_SPLITS)

kernel_name = "hybrid_fnet_retention_shortconv_encoder"


def _layer_norm(x, gain, bias):
    xf = x.astype(jnp.float32)
    mu = jnp.mean(xf, axis=-1, keepdims=True)
    xc = xf - mu
    var = jnp.mean(xc * xc, axis=-1, keepdims=True)
    y = xc * lax.rsqrt(var + LN_EPS) * gain.astype(jnp.float32) + bias.astype(jnp.float32)
    return y.astype(x.dtype)


def _rope_tables(seq_len):
    inv_freq = 1.0 / (ROPE_BASE ** jnp.linspace(0.0, 1.0, RET_HEAD_QK // 2, dtype=jnp.float32))
    ang = jnp.arange(seq_len, dtype=jnp.float32)[:, None] * inv_freq[None, :]
    return jnp.cos(ang)[:, None, :], jnp.sin(ang)[:, None, :]


def _rope(t, cos, sin):
    t1, t2 = jnp.split(t, 2, axis=-1)
    return jnp.concatenate([t1 * cos - t2 * sin, t1 * sin + t2 * cos], axis=-1)


def _fourier_mix(u):
    B, S, _ = u.shape
    ug = u.astype(jnp.float32).reshape(B, S, FNET_GROUPS, FNET_GROUP_DIM)
    f = jnp.fft.fft2(ug, axes=(1, 3), norm="ortho").real
    return f.reshape(B, S, FNET_WIDTH).astype(u.dtype)


def _retention_causal(qc, kc, vc, log_gamma, strict):
    _, B, H, C, dk = qc.shape
    dv = vc.shape[-1]
    pos = jnp.arange(C, dtype=jnp.float32)
    diff = pos[:, None] - pos[None, :]
    mask = (diff > 0) if strict else (diff >= 0)
    lg = log_gamma[:, None, None]
    d_intra = jnp.where(mask, jnp.exp(jnp.where(mask, diff, 0.0) * lg), 0.0)
    d_q = jnp.exp((pos + 1.0) * lg[:, :, 0])[:, :, None]
    d_k = jnp.exp((C - 1.0 - pos) * lg[:, :, 0])[:, :, None]
    d_chunk = jnp.exp(C * log_gamma)[:, None, None]

    def step(state, blk):
        q, k, v = blk
        scores = jnp.einsum("bhid,bhjd->bhij", q, k) * d_intra
        o = (jnp.einsum("bhij,bhjv->bhiv", scores, v)
             + jnp.einsum("bhid,bhdv->bhiv", q * d_q, state))
        state = d_chunk * state + jnp.einsum("bhjd,bhjv->bhdv", k * d_k, v)
        return state, o

    state0 = jnp.zeros((B, H, dk, dv), jnp.float32)
    _, out = lax.scan(step, state0, (qc, kc, vc))
    return out


def _retention(q, k, v, g, decay_logit, cos, sin):
    B, S, _ = q.shape
    dt = q.dtype
    nc = S // RET_CHUNK
    qh = _rope(q.astype(jnp.float32).reshape(B, S, RET_HEADS, RET_HEAD_QK), cos, sin)
    kh = _rope(k.astype(jnp.float32).reshape(B, S, RET_HEADS, RET_HEAD_QK), cos, sin) * (RET_HEAD_QK ** -0.5)
    vh = v.astype(jnp.float32).reshape(B, S, RET_HEADS, RET_HEAD_V)

    def chunked(t):
        return t.reshape(B, nc, RET_CHUNK, RET_HEADS, t.shape[-1]).transpose(1, 0, 3, 2, 4)

    qc, kc, vc = chunked(qh), chunked(kh), chunked(vh)
    log_gamma = jax.nn.log_sigmoid(decay_logit.astype(jnp.float32))

    def flip(t):
        return jnp.flip(t, axis=(0, 3))

    fwd = _retention_causal(qc, kc, vc, log_gamma[0], strict=False)
    bwd = flip(_retention_causal(flip(qc), flip(kc), flip(vc), log_gamma[1], strict=True))
    o = (fwd + bwd).transpose(1, 0, 3, 2, 4).reshape(B, S, RET_HEADS, RET_HEAD_V)
    mu = jnp.mean(o, axis=-1, keepdims=True)
    oc = o - mu
    o = oc * lax.rsqrt(jnp.mean(oc * oc, axis=-1, keepdims=True) + HEAD_NORM_EPS)
    o = o.reshape(B, S, RET_V)
    return (jax.nn.silu(g.astype(jnp.float32)) * o).astype(dt)


def _short_conv(b, c, xv, conv_w):
    u = c * xv
    up = jnp.pad(u, ((0, 0), (1, 1), (0, 0)))
    y = conv_w[0] * up[:, :-2] + conv_w[1] * up[:, 1:-1] + conv_w[2] * up[:, 2:]
    return b * y


def _mixer(x, w_in_l, decay_logit_l, conv_w_l, w_f_out, w_r_out, w_c_out, w_o_l, cos, sin):
    (wf, wq, wk, wv, wg, wcb, wcc, wcx, wgf, wgr, wgc) = jnp.split(
        w_in_l, np.cumsum(IN_SPLITS)[:-1], axis=1)

    def proj(w):
        return jnp.einsum("bsd,de->bse", x, w)

    f = jnp.einsum("bse,ed->bsd", _fourier_mix(proj(wf)), w_f_out)
    r = jnp.einsum("bse,ed->bsd",
                   _retention(proj(wq), proj(wk), proj(wv), proj(wg), decay_logit_l, cos, sin), w_r_out)
    c = jnp.einsum("bse,ed->bsd", _short_conv(proj(wcb), proj(wcc), proj(wcx), conv_w_l), w_c_out)
    merged = (jax.nn.sigmoid(proj(wgf)) * f
              + jax.nn.sigmoid(proj(wgr)) * r
              + jax.nn.sigmoid(proj(wgc)) * c)
    return jnp.einsum("bsd,de->bse", merged, w_o_l)


def _ffn(x, w_ffn_in_l, w_ffn_out_l):
    gate, up = jnp.split(jnp.einsum("bsd,df->bsf", x, w_ffn_in_l), 2, axis=-1)
    return jnp.einsum("bsf,fd->bsd", jax.nn.silu(gate) * up, w_ffn_out_l)


def _trunk(x, w_in, ret_decay_logit, conv_w, w_fourier_out, w_ret_out, w_conv_out, w_o,
           ln_gain, ln_bias, w_ffn_in, w_ffn_out):
    cos, sin = _rope_tables(x.shape[1])
    for l in range(DEPTH):
        mix = _mixer(x, w_in[l], ret_decay_logit[l], conv_w[l], w_fourier_out[l], w_ret_out[l],
                     w_conv_out[l], w_o[l], cos, sin)
        x = _layer_norm(DEEPNORM_ALPHA * x + mix, ln_gain[l, 0], ln_bias[l, 0])
        x = _layer_norm(DEEPNORM_ALPHA * x + _ffn(x, w_ffn_in[l], w_ffn_out[l]), ln_gain[l, 1], ln_bias[l, 1])
    return x


def setup_inputs(seed: int = 0) -> dict:
    key = jax.random.key(seed)
    ks = jax.random.split(key, 14)
    f32 = jnp.float32
    gamma = 1.0 - 2.0 ** (-5.0 - jnp.arange(RET_HEADS, dtype=f32))
    base_logit = jnp.log(gamma) - jnp.log1p(-gamma)
    return {
        "x_prompt": jax.random.normal(ks[0], (BATCH, SEQ, D_MODEL), f32),
        "x_sample": jax.random.normal(ks[1], (DEC_BATCH, DEC_SEQ, D_MODEL), f32),
        "w_in": jax.random.normal(ks[2], (DEPTH, D_MODEL, IN_COLS), f32) * D_MODEL ** -0.5,
        "ret_decay_logit": base_logit[None, None, :] + 0.1 * jax.random.normal(ks[3], (DEPTH, 2, RET_HEADS), f32),
        "conv_w": jax.random.normal(ks[4], (DEPTH, CONV_K, CONV_WIDTH), f32) * CONV_K ** -0.5,
        "w_fourier_out": jax.random.normal(ks[5], (DEPTH, FNET_WIDTH, D_MODEL), f32) * FNET_WIDTH ** -0.5,
        "w_ret_out": jax.random.normal(ks[6], (DEPTH, RET_V, D_MODEL), f32) * RET_V ** -0.5,
        "w_conv_out": jax.random.normal(ks[7], (DEPTH, CONV_WIDTH, D_MODEL), f32) * CONV_WIDTH ** -0.5,
        "w_o": jax.random.normal(ks[8], (DEPTH, D_MODEL, D_MODEL), f32) * (D_MODEL ** -0.5 * DEEPNORM_BETA),
        "ln_gain": 1.0 + 0.02 * jax.random.normal(ks[9], (DEPTH, 2, D_MODEL), f32),
        "ln_bias": 0.02 * jax.random.normal(ks[10], (DEPTH, 2, D_MODEL), f32),
        "w_ffn_in": jax.random.normal(ks[11], (DEPTH, D_MODEL, 2 * FFN_HIDDEN), f32) * D_MODEL ** -0.5,
        "w_ffn_out": jax.random.normal(ks[12], (DEPTH, FFN_HIDDEN, D_MODEL), f32) * (FFN_HIDDEN ** -0.5 * DEEPNORM_BETA),
    }


def reference(x_prompt, x_sample, w_in, ret_decay_logit, conv_w, w_fourier_out, w_ret_out,
              w_conv_out, w_o, ln_gain, ln_bias, w_ffn_in, w_ffn_out):
    y_prompt = _trunk(x_prompt, w_in, ret_decay_logit, conv_w, w_fourier_out, w_ret_out, w_conv_out,
                      w_o, ln_gain, ln_bias, w_ffn_in, w_ffn_out)
    y_sample = _trunk(x_sample, w_in, ret_decay_logit, conv_w, w_fourier_out, w_ret_out, w_conv_out,
                      w_o, ln_gain, ln_bias, w_ffn_in, w_ffn_out)
    return (y_prompt, y_sample)
```

```python
import functools
import math

import numpy as np
import jax
import jax.numpy as jnp
from jax import lax
from jax.experimental import pallas as pl
from jax.experimental.pallas import tpu as pltpu

D_MODEL = 1024
DEPTH = 4
FNET_GROUPS = 4
FNET_GROUP_DIM = D_MODEL // FNET_GROUPS
RET_HEAD_QK = 256
RET_HEADS = D_MODEL // RET_HEAD_QK
RET_HEAD_V = 2 * RET_HEAD_QK
RET_QK = RET_HEADS * RET_HEAD_QK
RET_V = RET_HEADS * RET_HEAD_V
ROPE_BASE = 10000.0
FFN_HIDDEN = -(-8 * D_MODEL // (3 * 256)) * 256
DEEPNORM_ALPHA = (2.0 * DEPTH) ** 0.25
LN_EPS = 1e-5
HEAD_NORM_EPS = 1e-6
IN_COLS = 13 * D_MODEL

COL_F = 0
COL_QK = 1
COL_REST = 3
R_V, R_G, R_CB, R_CC, R_CX, R_GF, R_GR, R_GC = 0, 2, 4, 5, 6, 7, 8, 9
REST_COLS = 10 * D_MODEL

MXU_DIM = 256
FFT_N1 = 256
FFT_KH = 128
RET_SUB = 256
VMEM_LIMIT = 56 * 1024 * 1024

BF16 = jnp.bfloat16
F32 = jnp.float32


def _cparams(n_axes, vmem=VMEM_LIMIT):
    return pltpu.CompilerParams(
        dimension_semantics=("arbitrary",) * n_axes, vmem_limit_bytes=vmem)


def _proj_kernel(x_ref, w_ref, o_ref):
    o_ref[...] = jnp.dot(x_ref[...], w_ref[...],
                         preferred_element_type=F32).astype(o_ref.dtype)


def _proj(x, w_in, layer, col0, ncols, out_dtype, tm=1024, tn=1024):
    T = x.shape[0]
    return pl.pallas_call(
        _proj_kernel,
        grid=(ncols // tn, T // tm),
        in_specs=[pl.BlockSpec((tm, D_MODEL), lambda j, i: (i, 0)),
                  pl.BlockSpec((None, D_MODEL, tn), lambda j, i: (layer, 0, col0 + j))],
        out_specs=pl.BlockSpec((tm, tn), lambda j, i: (i, j)),
        out_shape=jax.ShapeDtypeStruct((T, ncols), out_dtype),
        compiler_params=_cparams(2),
        name="proj",
    )(x, w_in)


def _proj_rope_kernel(x_ref, w_ref, cos_ref, sin_ref, o_ref):
    j = pl.program_id(0)
    acc = jnp.dot(x_ref[...], w_ref[...], preferred_element_type=F32)
    scale = jnp.where(j == 1, RET_HEAD_QK ** -0.5, 1.0).astype(F32)
    cos = cos_ref[...] * scale
    sin = sin_ref[...] * scale
    half = RET_HEAD_QK // 2
    for h in range(RET_HEADS):
        lo = h * RET_HEAD_QK
        t1 = acc[:, lo:lo + half]
        t2 = acc[:, lo + half:lo + 2 * half]
        o_ref[:, lo:lo + half] = (t1 * cos - t2 * sin).astype(o_ref.dtype)
        o_ref[:, lo + half:lo + 2 * half] = (t1 * sin + t2 * cos).astype(o_ref.dtype)


def _proj_rope(x, w_in, layer, cos, sin, seq, tm=1024):
    T = x.shape[0]
    tn = RET_QK
    tm = min(tm, seq)
    sblk = seq // tm
    return pl.pallas_call(
        _proj_rope_kernel,
        grid=(2, T // tm),
        in_specs=[pl.BlockSpec((tm, D_MODEL), lambda j, i: (i, 0)),
                  pl.BlockSpec((None, D_MODEL, tn), lambda j, i: (layer, 0, COL_QK + j)),
                  pl.BlockSpec((tm, RET_HEAD_QK // 2), lambda j, i: (i % sblk, 0)),
                  pl.BlockSpec((tm, RET_HEAD_QK // 2), lambda j, i: (i % sblk, 0))],
        out_specs=pl.BlockSpec((tm, tn), lambda j, i: (i, j)),
        out_shape=jax.ShapeDtypeStruct((T, 2 * tn), BF16),
        compiler_params=_cparams(2),
        name="proj_rope",
    )(x, w_in, cos, sin)


def _fourier_tables(seq):
    n1n, khn = FFT_N1, FFT_KH
    n2n = seq // n1n
    r = MXU_DIM // n2n
    k1 = np.arange(n1n, dtype=np.int64)
    n1 = np.arange(n1n, dtype=np.int64)
    a1 = np.zeros((n2n, n1n // khn, 2 * khn, n1n), np.float64)
    for n2 in range(n2n):
        ph = (k1[:, None] * (n2n * n1[None, :] + n2)) % seq
        ang = 2.0 * np.pi * ph / seq
        c, s = np.cos(ang), -np.sin(ang)
        for hh in range(n1n // khn):
            a1[n2, hh, :khn] = c[hh * khn:(hh + 1) * khn]
            a1[n2, hh, khn:] = s[hh * khn:(hh + 1) * khn]
    k2 = np.arange(n2n, dtype=np.int64)
    ang2 = 2.0 * np.pi * ((k2[:, None] * k2[None, :]) % n2n) / n2n
    eye = np.eye(r)
    gr = np.kron(np.cos(ang2), eye)
    gi = np.kron(-np.sin(ang2), eye)
    m3 = np.block([[gr, -gi], [gi, gr]])
    cg = np.arange(FNET_GROUP_DIM, dtype=np.int64)
    angg = 2.0 * np.pi * ((cg[:, None] * cg[None, :]) % FNET_GROUP_DIM) / FNET_GROUP_DIM
    a4 = np.concatenate([np.cos(angg), np.sin(angg)], axis=0)
    return (jnp.asarray(a1, BF16), jnp.asarray(m3, BF16), jnp.asarray(a4, BF16))


def _fourier_kernel(xa_ref, xb_ref, a1_ref, m3_ref, a4_ref, o_ref, y_ref, *, n2n, r, scale):
    n1n, khn = FFT_N1, FFT_KH
    for n2 in range(n2n):
        xs = jnp.concatenate([xa_ref[pl.ds(n2, n1n, stride=n2n), :],
                              xb_ref[pl.ds(n2, n1n, stride=n2n), :]], axis=1).astype(BF16)
        y = jnp.dot(a1_ref[n2], xs, preferred_element_type=F32)
        y_ref[0, n2] = y[:khn]
        y_ref[1, n2] = y[khn:]

    m3 = m3_ref[...]
    a4 = a4_ref[...]

    def body(j, carry):
        r0 = pl.multiple_of(j * r, r)
        yr = y_ref[0, :, pl.ds(r0, r), :].reshape(MXU_DIM, FNET_GROUP_DIM)
        yi = y_ref[1, :, pl.ds(r0, r), :].reshape(MXU_DIM, FNET_GROUP_DIM)
        v = jnp.concatenate([yr, yi], axis=0).astype(BF16)
        z = jnp.dot(m3, v, preferred_element_type=F32)
        zz = jnp.concatenate([z[:MXU_DIM], z[MXU_DIM:]], axis=1).astype(BF16)
        o = jnp.dot(zz, a4, preferred_element_type=F32) * scale
        o_ref[:, pl.ds(r0, r), :] = o.reshape(n2n, r, FNET_GROUP_DIM).astype(o_ref.dtype)
        return carry

    lax.fori_loop(0, khn // r, body, 0)


def _fourier(u, tables, batch, seq):
    a1, m3, a4 = tables
    n1n, khn = FFT_N1, FFT_KH
    n2n = seq // n1n
    r = MXU_DIM // n2n
    g = FNET_GROUP_DIM
    kern = functools.partial(_fourier_kernel, n2n=n2n, r=r,
                             scale=1.0 / math.sqrt(seq * g))
    out = pl.pallas_call(
        kern,
        grid=(batch, FNET_GROUPS, n1n // khn),
        in_specs=[pl.BlockSpec((None, seq, g // 2), lambda b, c, hh: (b, 0, 2 * c)),
                  pl.BlockSpec((None, seq, g // 2), lambda b, c, hh: (b, 0, 2 * c + 1)),
                  pl.BlockSpec((n2n, None, 2 * khn, n1n), lambda b, c, hh: (0, hh, 0, 0)),
                  pl.BlockSpec((2 * MXU_DIM, 2 * MXU_DIM), lambda b, c, hh: (0, 0)),
                  pl.BlockSpec((2 * g, g), lambda b, c, hh: (0, 0))],
        out_specs=pl.BlockSpec((None, n2n, khn, g), lambda b, c, hh: (b, 0, hh, c)),
        out_shape=jax.ShapeDtypeStruct((batch, n2n, n1n, D_MODEL), F32),
        scratch_shapes=[pltpu.VMEM((2, n2n, khn, g), F32)],
        compiler_params=_cparams(3),
        name="fourier",
    )(u.reshape(batch, seq, D_MODEL), u.reshape(batch, seq, D_MODEL), a1, m3, a4)
    return out.reshape(batch * seq, D_MODEL)


def _ret_kernel(lg_ref, q_ref, k_ref, v_ref, g_ref, o_ref,
                sb_ref, st_ref, dec_ref, dmat_ref, *, nb, nsub):
    C = RET_SUB
    h = pl.program_id(1)
    t = pl.program_id(2)
    lgf = lg_ref[0, h]
    lgb = lg_ref[1, h]

    @pl.when(t == 0)
    def _():
        ii = lax.broadcasted_iota(jnp.int32, (C, RET_HEAD_QK), 0).astype(F32)
        dec_ref[0] = jnp.exp((ii + 1.0) * lgf)
        dec_ref[1] = jnp.exp((C - ii) * lgb)
        dec_ref[2] = jnp.exp((C - 1.0 - ii) * lgf)
        dec_ref[3] = jnp.exp(ii * lgb)
        ri = lax.broadcasted_iota(jnp.int32, (C, C), 0)
        ci = lax.broadcasted_iota(jnp.int32, (C, C), 1)
        dif = (ri - ci).astype(F32)
        dmat_ref[...] = jnp.where(ri >= ci, jnp.exp(jnp.maximum(dif, 0.0) * lgf),
                                  jnp.exp(jnp.maximum(-dif, 0.0) * lgb))
        st_ref[...] = jnp.zeros_like(st_ref)

    @pl.when(t == nb)
    def _():
        st_ref[...] = jnp.zeros_like(st_ref)

    def kv_update(kc, vc, dk, chunk_decay):
        kd = (kc.astype(F32) * dk).astype(BF16)
        upd = lax.dot_general(kd, vc, (((0,), (0,)), ((), ())),
                              preferred_element_type=F32)
        st_ref[...] = st_ref[...] * chunk_decay + upd

    @pl.when(t < nb)
    def _():
        blk = nb - 1 - t
        decay = jnp.exp(jnp.full((1, RET_HEAD_V), C * lgb, F32))

        def body(s, carry):
            c = nsub - 1 - s
            rows = pl.ds(pl.multiple_of(c * C, C), C)
            sb_ref[blk * nsub + c] = st_ref[...].astype(BF16)
            kv_update(k_ref[rows, :], v_ref[rows, :], dec_ref[3], decay)
            return carry

        lax.fori_loop(0, nsub, body, 0)

    @pl.when(t >= nb)
    def _():
        blk = t - nb
        decay = jnp.exp(jnp.full((1, RET_HEAD_V), C * lgf, F32))

        def body(c, carry):
            rows = pl.ds(pl.multiple_of(c * C, C), C)
            qc = q_ref[rows, :]
            kc = k_ref[rows, :]
            vc = v_ref[rows, :]
            s = lax.dot_general(qc, kc, (((1,), (1,)), ((), ())),
                                preferred_element_type=F32)
            p = (s * dmat_ref[...]).astype(BF16)
            o = jnp.dot(p, vc, preferred_element_type=F32)
            qf = qc.astype(F32)
            qq = jnp.concatenate([(qf * dec_ref[0]).astype(BF16),
                                  (qf * dec_ref[1]).astype(BF16)], axis=1)
            ss = jnp.concatenate([st_ref[...].astype(BF16),
                                  sb_ref[blk * nsub + c]], axis=0)
            o = o + jnp.dot(qq, ss, preferred_element_type=F32)
            mu = jnp.mean(o, axis=-1, keepdims=True)
            oc = o - mu
            on = oc * lax.rsqrt(jnp.mean(oc * oc, axis=-1, keepdims=True) + HEAD_NORM_EPS)
            gg = g_ref[rows, :].astype(F32)
            o_ref[rows, :] = (gg * (1.0 / (1.0 + jnp.exp(-gg))) * on).astype(o_ref.dtype)
            kv_update(kc, vc, dec_ref[2], decay)
            return carry

        lax.fori_loop(0, nsub, body, 0)


def _retention(qk, rest, log_gamma, batch, seq, cb=2048):
    cb = min(cb, seq)
    nb = seq // cb
    nsub = cb // RET_SUB
    T = batch * seq
    H = RET_HEADS

    def qmap(b, h, t, lg):
        return (b * nb + jnp.maximum(t - nb, 0), h)

    def kmap(b, h, t, lg):
        return (b * nb + jnp.where(t < nb, nb - 1 - t, t - nb), H + h)

    def vmap_(b, h, t, lg):
        return (b * nb + jnp.where(t < nb, nb - 1 - t, t - nb), (R_V * D_MODEL) // RET_HEAD_V + h)

    def gmap(b, h, t, lg):
        return (b * nb + jnp.maximum(t - nb, 0), (R_G * D_MODEL) // RET_HEAD_V + h)

    grid_spec = pltpu.PrefetchScalarGridSpec(
        num_scalar_prefetch=1,
        grid=(batch, H, 2 * nb),
        in_specs=[pl.BlockSpec((cb, RET_HEAD_QK), qmap),
                  pl.BlockSpec((cb, RET_HEAD_QK), kmap),
                  pl.BlockSpec((cb, RET_HEAD_V), vmap_),
                  pl.BlockSpec((cb, RET_HEAD_V), gmap)],
        out_specs=pl.BlockSpec((cb, RET_HEAD_V), qmap),
        scratch_shapes=[pltpu.VMEM((seq // RET_SUB, RET_HEAD_QK, RET_HEAD_V), BF16),
                        pltpu.VMEM((RET_HEAD_QK, RET_HEAD_V), F32),
                        pltpu.VMEM((4, RET_SUB, RET_HEAD_QK), F32),
                        pltpu.VMEM((RET_SUB, RET_SUB), F32)],
    )
    return pl.pallas_call(
        functools.partial(_ret_kernel, nb=nb, nsub=nsub),
        grid_spec=grid_spec,
        out_shape=jax.ShapeDtypeStruct((T, RET_V), BF16),
        compiler_params=_cparams(3),
        name="retention",
    )(log_gamma, qk, qk, rest, rest)


def _layer_norm(y, gain, bias):
    mu = jnp.mean(y, axis=-1, keepdims=True)
    yc = y - mu
    var = jnp.mean(yc * yc, axis=-1, keepdims=True)
    return yc * lax.rsqrt(var + LN_EPS) * gain + bias


def _sigmoid(z):
    return 1.0 / (1.0 + jnp.exp(-z))


def _merge_kernel(x_ref, f_ref, r_ref, cb_ref, cc_ref, cx_ref,
                  ccp_ref, cxp_ref, ccn_ref, cxn_ref,
                  gf_ref, gr_ref, gc_ref,
                  wf_ref, wr_ref, wc_ref, wo_ref, cw_ref, ln_ref,
                  o_ref, *, tm, seq):
    i = pl.program_id(0)
    pos0 = (i * tm) % seq
    u = cc_ref[...].astype(F32) * cx_ref[...].astype(F32)
    keep_p = jnp.where(pos0 == 0, 0.0, 1.0).astype(F32)
    keep_n = jnp.where(pos0 + tm == seq, 0.0, 1.0).astype(F32)
    up_row = ccp_ref[7:8, :].astype(F32) * cxp_ref[7:8, :].astype(F32) * keep_p
    un_row = ccn_ref[0:1, :].astype(F32) * cxn_ref[0:1, :].astype(F32) * keep_n
    rows = lax.broadcasted_iota(jnp.int32, u.shape, 0)
    u_prev = jnp.where(rows == 0, up_row, pltpu.roll(u, 1, 0))
    u_next = jnp.where(rows == tm - 1, un_row, pltpu.roll(u, tm - 1, 0))
    y = cw_ref[0:1, :] * u_prev + cw_ref[1:2, :] * u + cw_ref[2:3, :] * u_next
    cin = (cb_ref[...].astype(F32) * y).astype(BF16)

    f = jnp.dot(f_ref[...].astype(BF16), wf_ref[...], preferred_element_type=F32)
    merged = _sigmoid(gf_ref[...].astype(F32)) * f
    rr = jnp.dot(r_ref[...], wr_ref[...], preferred_element_type=F32)
    merged = merged + _sigmoid(gr_ref[...].astype(F32)) * rr
    c = jnp.dot(cin, wc_ref[...], preferred_element_type=F32)
    merged = merged + _sigmoid(gc_ref[...].astype(F32)) * c
    mix = jnp.dot(merged.astype(BF16), wo_ref[...], preferred_element_type=F32)
    o_ref[...] = _layer_norm(DEEPNORM_ALPHA * x_ref[...] + mix, ln_ref[0:1, :], ln_ref[1:2, :])


def _merge(x, fmix, rin, rest, w_f, w_r, w_c, w_o, conv_w, ln, layer, seq, tm=256):
    T = x.shape[0]
    nb8 = T // 8
    t8 = tm // 8

    def col(cblk):
        return pl.BlockSpec((tm, D_MODEL), lambda i: (i, cblk))

    def prev(cblk):
        return pl.BlockSpec((8, D_MODEL), lambda i: (jnp.maximum(i * t8 - 1, 0), cblk))

    def nxt(cblk):
        return pl.BlockSpec((8, D_MODEL), lambda i: (jnp.minimum((i + 1) * t8, nb8 - 1), cblk))

    def wspec(rows):
        return pl.BlockSpec((None, rows, D_MODEL), lambda i: (layer, 0, 0))

    return pl.pallas_call(
        functools.partial(_merge_kernel, tm=tm, seq=seq),
        grid=(T // tm,),
        in_specs=[col(0), col(0),
                  pl.BlockSpec((tm, RET_V), lambda i: (i, 0)),
                  col(R_CB), col(R_CC), col(R_CX),
                  prev(R_CC), prev(R_CX), nxt(R_CC), nxt(R_CX),
                  col(R_GF), col(R_GR), col(R_GC),
                  wspec(D_MODEL), wspec(RET_V), wspec(D_MODEL), wspec(D_MODEL),
                  pl.BlockSpec((None, 8, D_MODEL), lambda i: (layer, 0, 0)),
                  pl.BlockSpec((None, 8, D_MODEL), lambda i: (layer, 0, 0))],
        out_specs=pl.BlockSpec((tm, D_MODEL), lambda i: (i, 0)),
        out_shape=jax.ShapeDtypeStruct((T, D_MODEL), F32),
        compiler_params=_cparams(1),
        name="merge",
    )(x, fmix, rin, rest, rest, rest, rest, rest, rest, rest, rest, rest, rest,
      w_f, w_r, w_c, w_o, conv_w, ln)


def _ffn_kernel(x_ref, w1_ref, w2_ref, ln_ref, o_ref, ob_ref):
    x = x_ref[...]
    h = jnp.dot(x.astype(BF16), w1_ref[...], preferred_element_type=F32)
    gate = h[:, :FFN_HIDDEN]
    up = h[:, FFN_HIDDEN:]
    a = (gate * _sigmoid(gate) * up).astype(BF16)
    y = jnp.dot(a, w2_ref[...], preferred_element_type=F32)
    out = _layer_norm(DEEPNORM_ALPHA * x + y, ln_ref[0:1, :], ln_ref[1:2, :])
    o_ref[...] = out
    ob_ref[...] = out.astype(BF16)


def _ffn(x, w1, w2, ln, layer, tm=256):
    T = x.shape[0]
    return pl.pallas_call(
        _ffn_kernel,
        grid=(T // tm,),
        in_specs=[pl.BlockSpec((tm, D_MODEL), lambda i: (i, 0)),
                  pl.BlockSpec((None, D_MODEL, 2 * FFN_HIDDEN), lambda i: (layer, 0, 0)),
                  pl.BlockSpec((None, FFN_HIDDEN, D_MODEL), lambda i: (layer, 0, 0)),
                  pl.BlockSpec((None, 8, D_MODEL), lambda i: (layer, 0, 0))],
        out_specs=[pl.BlockSpec((tm, D_MODEL), lambda i: (i, 0)),
                   pl.BlockSpec((tm, D_MODEL), lambda i: (i, 0))],
        out_shape=[jax.ShapeDtypeStruct((T, D_MODEL), F32),
                   jax.ShapeDtypeStruct((T, D_MODEL), BF16)],
        compiler_params=_cparams(1),
        name="ffn",
    )(x, w1, w2, ln)


def _rope_tables(seq):
    inv_freq = 1.0 / (ROPE_BASE ** jnp.linspace(0.0, 1.0, RET_HEAD_QK // 2, dtype=F32))
    ang = jnp.arange(seq, dtype=F32)[:, None] * inv_freq[None, :]
    return jnp.cos(ang), jnp.sin(ang)


def _pad8(a):
    return jnp.pad(a, ((0, 0), (0, 8 - a.shape[1]), (0, 0)))


def _trunk(x, p):
    batch, seq, _ = x.shape
    T = batch * seq
    cos, sin = _rope_tables(seq)
    ftab = _fourier_tables(seq)
    xf = x.reshape(T, D_MODEL)
    xb = xf.astype(BF16)
    for l in range(DEPTH):
        u = _proj(xb, p["w_in"], l, COL_F, D_MODEL, F32)
        qk = _proj_rope(xb, p["w_in"], l, cos, sin, seq)
        rest = _proj(xb, p["w_in"], l, COL_REST, REST_COLS, BF16)
        fmix = _fourier(u, ftab, batch, seq)
        rin = _retention(qk, rest, p["log_gamma"][l], batch, seq)
        x1 = _merge(xf, fmix, rin, rest, p["w_f"], p["w_r"], p["w_c"], p["w_o"],
                    p["conv_w"], p["ln1"], l, seq)
        xf, xb = _ffn(x1, p["w_ffn_in"], p["w_ffn_out"], p["ln2"], l)
    return xf.reshape(batch, seq, D_MODEL)


def kernel(x_prompt, x_sample, w_in, ret_decay_logit, conv_w, w_fourier_out, w_ret_out,
           w_conv_out, w_o, ln_gain, ln_bias, w_ffn_in, w_ffn_out):
    p = {
        "w_in": w_in.astype(BF16),
        "log_gamma": jax.nn.log_sigmoid(ret_decay_logit.astype(F32)),
        "conv_w": _pad8(conv_w.astype(F32)),
        "w_f": w_fourier_out.astype(BF16),
        "w_r": w_ret_out.astype(BF16),
        "w_c": w_conv_out.astype(BF16),
        "w_o": w_o.astype(BF16),
        "ln1": _pad8(jnp.stack([ln_gain[:, 0], ln_bias[:, 0]], axis=1).astype(F32)),
        "ln2": _pad8(jnp.stack([ln_gain[:, 1], ln_bias[:, 1]], axis=1).astype(F32)),
        "w_ffn_in": w_ffn_in.astype(BF16),
        "w_ffn_out": w_ffn_out.astype(BF16),
    }
    return (_trunk(x_prompt, p), _trunk(x_sample, p))
```

```python
import functools
import math

import numpy as np
import jax
import jax.numpy as jnp
from jax import lax
from jax.experimental import pallas as pl
from jax.experimental.pallas import tpu as pltpu

D_MODEL = 1024
DEPTH = 4
FNET_GROUPS = 4
FNET_GROUP_DIM = D_MODEL // FNET_GROUPS
RET_HEAD_QK = 256
RET_HEADS = D_MODEL // RET_HEAD_QK
RET_HEAD_V = 2 * RET_HEAD_QK
RET_QK = RET_HEADS * RET_HEAD_QK
RET_V = RET_HEADS * RET_HEAD_V
ROPE_BASE = 10000.0
FFN_HIDDEN = -(-8 * D_MODEL // (3 * 256)) * 256
DEEPNORM_ALPHA = (2.0 * DEPTH) ** 0.25
LN_EPS = 1e-5
HEAD_NORM_EPS = 1e-6
IN_COLS = 13 * D_MODEL

COL_F = 0
COL_QK = 1
COL_REST = 3
R_V, R_G, R_CB, R_CC, R_CX, R_GF, R_GR, R_GC = 0, 2, 4, 5, 6, 7, 8, 9
REST_COLS = 10 * D_MODEL

MXU_DIM = 256
FFT_N1 = 256
FFT_ROWS = 8
FFT_SLABS_PER_STEP = 8
FFT_UNROLL = 4
FFT_CH_TILES = 4
RET_SUB = 256
RET_UNROLL = 4
VMEM_LIMIT = 56 * 1024 * 1024

BF16 = jnp.bfloat16
F32 = jnp.float32


def _cparams(n_axes, vmem=VMEM_LIMIT):
    return pltpu.CompilerParams(
        dimension_semantics=("arbitrary",) * n_axes, vmem_limit_bytes=vmem)


def _proj_kernel(x_ref, w_ref, o_ref):
    o_ref[...] = jnp.dot(x_ref[...], w_ref[...],
                         preferred_element_type=F32).astype(o_ref.dtype)


def _proj(x, w_in, layer, col0, ncols, out_dtype, tm=2048, tn=1024):
    T = x.shape[0]
    tm = min(tm, T)
    return pl.pallas_call(
        _proj_kernel,
        grid=(ncols // tn, T // tm),
        in_specs=[pl.BlockSpec((tm, D_MODEL), lambda j, i: (i, 0)),
                  pl.BlockSpec((None, D_MODEL, tn), lambda j, i: (layer, 0, col0 + j))],
        out_specs=pl.BlockSpec((tm, tn), lambda j, i: (i, j)),
        out_shape=jax.ShapeDtypeStruct((T, ncols), out_dtype),
        compiler_params=_cparams(2),
        name="proj",
    )(x, w_in)


def _proj_rope_kernel(x_ref, w_ref, cos_ref, sin_ref, o_ref):
    j = pl.program_id(0)
    acc = jnp.dot(x_ref[...], w_ref[...], preferred_element_type=F32)
    scale = jnp.where(j == 1, RET_HEAD_QK ** -0.5, 1.0).astype(F32)
    cos = cos_ref[...] * scale
    sin = sin_ref[...] * scale
    half = RET_HEAD_QK // 2
    for h in range(RET_HEADS):
        lo = h * RET_HEAD_QK
        t1 = acc[:, lo:lo + half]
        t2 = acc[:, lo + half:lo + 2 * half]
        o_ref[:, lo:lo + half] = (t1 * cos - t2 * sin).astype(o_ref.dtype)
        o_ref[:, lo + half:lo + 2 * half] = (t1 * sin + t2 * cos).astype(o_ref.dtype)


def _proj_rope(x, w_in, layer, cos, sin, seq, tm=2048):
    T = x.shape[0]
    tn = RET_QK
    tm = min(tm, seq)
    sblk = seq // tm
    return pl.pallas_call(
        _proj_rope_kernel,
        grid=(2, T // tm),
        in_specs=[pl.BlockSpec((tm, D_MODEL), lambda j, i: (i, 0)),
                  pl.BlockSpec((None, D_MODEL, tn), lambda j, i: (layer, 0, COL_QK + j)),
                  pl.BlockSpec((tm, RET_HEAD_QK // 2), lambda j, i: (i % sblk, 0)),
                  pl.BlockSpec((tm, RET_HEAD_QK // 2), lambda j, i: (i % sblk, 0))],
        out_specs=pl.BlockSpec((tm, tn), lambda j, i: (i, j)),
        out_shape=jax.ShapeDtypeStruct((T, 2 * tn), BF16),
        compiler_params=_cparams(2),
        name="proj_rope",
    )(x, w_in, cos, sin)


def _fourier_tables(seq):
    n1n = FFT_N1
    n2n = seq // n1n
    r = FFT_ROWS
    k1 = np.arange(n1n, dtype=np.int64)
    n1 = np.arange(n1n, dtype=np.int64)
    a1 = np.zeros((n2n, 2 * n1n, n1n), np.float64)
    for n2 in range(n2n):
        ang = 2.0 * np.pi * ((k1[:, None] * (n2n * n1[None, :] + n2)) % seq) / seq
        a1[n2, :n1n] = np.cos(ang)
        a1[n2, n1n:] = -np.sin(ang)
    k2 = np.arange(n2n, dtype=np.int64)
    ang2 = 2.0 * np.pi * ((k2[:, None] * k2[None, :]) % n2n) / n2n
    eye = np.eye(r)
    gr = np.kron(np.cos(ang2), eye)
    gi = np.kron(-np.sin(ang2), eye)
    m3 = np.block([[gr, -gi], [gi, gr]])
    cg = np.arange(FNET_GROUP_DIM, dtype=np.int64)
    angg = 2.0 * np.pi * ((cg[:, None] * cg[None, :]) % FNET_GROUP_DIM) / FNET_GROUP_DIM
    a4 = np.concatenate([np.cos(angg), np.sin(angg)], axis=0)
    return (jnp.asarray(a1, BF16), jnp.asarray(m3, BF16), jnp.asarray(a4, BF16))


def _fourier_kernel(*refs, n2n, nps, scale):
    x_refs = refs[:nps]
    a1_ref, m3_ref, a4_ref, o_ref, y_ref, z_ref = refs[nps:]
    n1n, r, g = FFT_N1, FFT_ROWS, FNET_GROUP_DIM
    blk = n2n * r
    t = pl.program_id(2)
    for i in range(nps):
        xs = x_refs[i][...].astype(BF16)
        y = jnp.dot(a1_ref[i], xs, preferred_element_type=F32)
        y_ref[0, t * nps + i] = y[:n1n]
        y_ref[1, t * nps + i] = y[n1n:]

    @pl.when(t == n2n // nps - 1)
    def _():
        def stage2(j, carry):
            r0 = pl.multiple_of(j * r, r)
            yr = y_ref[0, :, pl.ds(r0, r), :].reshape(blk, g)
            yi = y_ref[1, :, pl.ds(r0, r), :].reshape(blk, g)
            v = jnp.concatenate([yr, yi], axis=0).astype(BF16)
            z = jnp.dot(m3_ref[...], v, preferred_element_type=F32)
            z_ref[j] = jnp.concatenate([z[:blk], z[blk:]], axis=1).astype(BF16)
            return carry

        lax.fori_loop(0, n1n // r, stage2, 0, unroll=FFT_UNROLL)

        def channels(jj, carry):
            zz = z_ref[pl.ds(jj * FFT_CH_TILES, FFT_CH_TILES)].reshape(FFT_CH_TILES * blk, 2 * g)
            o = jnp.dot(zz, a4_ref[...], preferred_element_type=F32) * scale
            for i in range(FFT_CH_TILES):
                r0 = pl.multiple_of((jj * FFT_CH_TILES + i) * r, r)
                o_ref[:, pl.ds(r0, r), :] = (
                    o[i * blk:(i + 1) * blk].reshape(n2n, r, g).astype(o_ref.dtype))
            return carry

        lax.fori_loop(0, n1n // r // FFT_CH_TILES, channels, 0, unroll=2)


def _fourier(u, tables, batch, seq):
    a1, m3, a4 = tables
    n1n, r, g = FFT_N1, FFT_ROWS, FNET_GROUP_DIM
    n2n = seq // n1n
    nps = min(FFT_SLABS_PER_STEP, n2n)
    blk = n2n * r
    kern = functools.partial(_fourier_kernel, n2n=n2n, nps=nps,
                             scale=1.0 / math.sqrt(seq * g))

    def xspec(i):
        return pl.BlockSpec((None, n1n, g),
                            lambda b, c, t: (b, 0, (t * nps + i) * FNET_GROUPS + c))

    uv = u.reshape(batch, n1n, n2n * D_MODEL)
    out = pl.pallas_call(
        kern,
        grid=(batch, FNET_GROUPS, n2n // nps),
        in_specs=[xspec(i) for i in range(nps)] + [
            pl.BlockSpec((nps, 2 * n1n, n1n), lambda b, c, t: (t, 0, 0)),
            pl.BlockSpec((2 * blk, 2 * blk), lambda b, c, t: (0, 0)),
            pl.BlockSpec((2 * g, g), lambda b, c, t: (0, 0))],
        out_specs=pl.BlockSpec((None, n2n, n1n, g), lambda b, c, t: (b, 0, 0, c)),
        out_shape=jax.ShapeDtypeStruct((batch, n2n, n1n, D_MODEL), F32),
        scratch_shapes=[pltpu.VMEM((2, n2n, n1n, g), F32),
                        pltpu.VMEM((n1n // r, blk, 2 * g), BF16)],
        compiler_params=_cparams(3),
        name="fourier",
    )(*([uv] * nps), a1, m3, a4)
    return out.reshape(batch * seq, D_MODEL)


def _ret_kernel(lg_ref, q_ref, k_ref, v_ref, g_ref, o_ref,
                sb_ref, st_ref, dec_ref, dmat_ref, oa_ref, ob_ref, pa_ref, pb_ref, *, nb, nsub):
    C = RET_SUB
    h = pl.program_id(1)
    t = pl.program_id(2)
    lgf = lg_ref[0, h]
    lgb = lg_ref[1, h]

    @pl.when(t == 0)
    def _():
        ii = lax.broadcasted_iota(jnp.int32, (C, RET_HEAD_QK), 0).astype(F32)
        dec_ref[0] = jnp.exp((ii + 1.0) * lgf).astype(BF16)
        dec_ref[1] = jnp.exp((C - ii) * lgb).astype(BF16)
        dec_ref[2] = jnp.exp((C - 1.0 - ii) * lgf).astype(BF16)
        dec_ref[3] = jnp.exp(ii * lgb).astype(BF16)
        ri = lax.broadcasted_iota(jnp.int32, (C, C), 0)
        ci = lax.broadcasted_iota(jnp.int32, (C, C), 1)
        dif = (ri - ci).astype(F32)
        dmat_ref[...] = jnp.where(ri >= ci, jnp.exp(jnp.maximum(dif, 0.0) * lgf),
                                  jnp.exp(jnp.maximum(-dif, 0.0) * lgb))
        st_ref[...] = jnp.zeros_like(st_ref)

    @pl.when(t == nb)
    def _():
        st_ref[...] = jnp.zeros_like(st_ref)

    def kv_update(kc, vc, dk, chunk_decay):
        upd = lax.dot_general(kc * dk, vc, (((0,), (0,)), ((), ())),
                              preferred_element_type=F32)
        st_ref[...] = st_ref[...] * chunk_decay + upd

    @pl.when(t < nb)
    def _():
        blk = nb - 1 - t
        decay = jnp.exp(jnp.full((1, RET_HEAD_V), C * lgb, F32))

        def body(s, carry):
            c = nsub - 1 - s
            rows = pl.ds(pl.multiple_of(c * C, C), C)
            sb_ref[blk * nsub + c] = st_ref[...].astype(BF16)
            kv_update(k_ref[rows, :], v_ref[rows, :], dec_ref[3], decay)
            return carry

        lax.fori_loop(0, nsub, body, 0, unroll=RET_UNROLL)

    @pl.when(t >= nb)
    def _():
        blk = t - nb
        decay = jnp.exp(jnp.full((1, RET_HEAD_V), C * lgf, F32))

        def rows_of(c):
            return pl.ds(pl.multiple_of(c * C, C), C)

        def scores(c, pbuf):
            rows = rows_of(c)
            s = lax.dot_general(q_ref[rows, :], k_ref[rows, :], (((1,), (1,)), ((), ())),
                                preferred_element_type=F32)
            pbuf[...] = (s * dmat_ref[...]).astype(BF16)

        def mix(c, pbuf, obuf):
            rows = rows_of(c)
            qc = q_ref[rows, :]
            vc = v_ref[rows, :]
            lhs = jnp.concatenate([pbuf[...], qc * dec_ref[0], qc * dec_ref[1]], axis=1)
            rhs = jnp.concatenate([vc, st_ref[...].astype(BF16), sb_ref[blk * nsub + c]], axis=0)
            obuf[...] = jnp.dot(lhs, rhs, preferred_element_type=F32)
            kv_update(k_ref[rows, :], vc, dec_ref[2], decay)

        def norm_gate(c, obuf):
            rows = rows_of(c)
            o = obuf[...]
            mu = jnp.mean(o, axis=-1, keepdims=True)
            oc = o - mu
            on = oc * lax.rsqrt(jnp.mean(oc * oc, axis=-1, keepdims=True) + HEAD_NORM_EPS)
            gg = g_ref[rows, :]
            silu = gg * (1.0 / (1.0 + jnp.exp(-gg)))
            o_ref[rows, :] = (on.astype(BF16) * silu).astype(o_ref.dtype)

        scores(0, pa_ref)
        scores(1, pb_ref)
        mix(0, pa_ref, oa_ref)

        def body(m, carry):
            c = 2 * m + 1
            scores(c + 1, pa_ref)
            mix(c, pb_ref, ob_ref)
            norm_gate(c - 1, oa_ref)
            scores(c + 2, pb_ref)
            mix(c + 1, pa_ref, oa_ref)
            norm_gate(c, ob_ref)
            return carry

        lax.fori_loop(0, nsub // 2 - 1, body, 0)
        mix(nsub - 1, pb_ref, ob_ref)
        norm_gate(nsub - 2, oa_ref)
        norm_gate(nsub - 1, ob_ref)


def _retention(qk, rest, log_gamma, batch, seq, cb=2048):
    cb = min(cb, seq)
    nb = seq // cb
    nsub = cb // RET_SUB
    T = batch * seq
    H = RET_HEADS

    def qmap(b, h, t, lg):
        return (b * nb + jnp.maximum(t - nb, 0), h)

    def kmap(b, h, t, lg):
        return (b * nb + jnp.where(t < nb, nb - 1 - t, t - nb), H + h)

    def vmap_(b, h, t, lg):
        return (b * nb + jnp.where(t < nb, nb - 1 - t, t - nb), (R_V * D_MODEL) // RET_HEAD_V + h)

    def gmap(b, h, t, lg):
        return (b * nb + jnp.maximum(t - nb, 0), (R_G * D_MODEL) // RET_HEAD_V + h)

    grid_spec = pltpu.PrefetchScalarGridSpec(
        num_scalar_prefetch=1,
        grid=(batch, H, 2 * nb),
        in_specs=[pl.BlockSpec((cb, RET_HEAD_QK), qmap),
                  pl.BlockSpec((cb, RET_HEAD_QK), kmap),
                  pl.BlockSpec((cb, RET_HEAD_V), vmap_),
                  pl.BlockSpec((cb, RET_HEAD_V), gmap)],
        out_specs=pl.BlockSpec((cb, RET_HEAD_V), qmap),
        scratch_shapes=[pltpu.VMEM((seq // RET_SUB, RET_HEAD_QK, RET_HEAD_V), BF16),
                        pltpu.VMEM((RET_HEAD_QK, RET_HEAD_V), F32),
                        pltpu.VMEM((4, RET_SUB, RET_HEAD_QK), BF16),
                        pltpu.VMEM((RET_SUB, RET_SUB), F32),
                        pltpu.VMEM((RET_SUB, RET_HEAD_V), F32),
                        pltpu.VMEM((RET_SUB, RET_HEAD_V), F32),
                        pltpu.VMEM((RET_SUB, RET_SUB), BF16),
                        pltpu.VMEM((RET_SUB, RET_SUB), BF16)],
    )
    return pl.pallas_call(
        functools.partial(_ret_kernel, nb=nb, nsub=nsub),
        grid_spec=grid_spec,
        out_shape=jax.ShapeDtypeStruct((T, RET_V), BF16),
        compiler_params=_cparams(3),
        name="retention",
    )(log_gamma, qk, qk, rest, rest)


def _layer_norm(y, gain, bias):
    mu = jnp.mean(y, axis=-1, keepdims=True)
    yc = y - mu
    var = jnp.mean(yc * yc, axis=-1, keepdims=True)
    return yc * lax.rsqrt(var + LN_EPS) * gain + bias


def _sigmoid(z):
    return 1.0 / (1.0 + jnp.exp(-z))


def _merge_kernel(x_ref, f_ref, r_ref, cb_ref, cc_ref, cx_ref,
                  ccp_ref, cxp_ref, ccn_ref, cxn_ref,
                  gf_ref, gr_ref, gc_ref,
                  wf_ref, wr_ref, wc_ref, wo_ref, cw_ref, ln_ref,
                  o_ref, *, tm, seq):
    i = pl.program_id(0)
    pos0 = (i * tm) % seq
    u = cc_ref[...].astype(F32) * cx_ref[...].astype(F32)
    keep_p = jnp.where(pos0 == 0, 0.0, 1.0).astype(F32)
    keep_n = jnp.where(pos0 + tm == seq, 0.0, 1.0).astype(F32)
    up_row = ccp_ref[7:8, :].astype(F32) * cxp_ref[7:8, :].astype(F32) * keep_p
    un_row = ccn_ref[0:1, :].astype(F32) * cxn_ref[0:1, :].astype(F32) * keep_n
    rows = lax.broadcasted_iota(jnp.int32, u.shape, 0)
    u_prev = jnp.where(rows == 0, up_row, pltpu.roll(u, 1, 0))
    u_next = jnp.where(rows == tm - 1, un_row, pltpu.roll(u, tm - 1, 0))
    y = cw_ref[0:1, :] * u_prev + cw_ref[1:2, :] * u + cw_ref[2:3, :] * u_next
    cin = (cb_ref[...].astype(F32) * y).astype(BF16)

    f = jnp.dot(f_ref[...].astype(BF16), wf_ref[...], preferred_element_type=F32)
    merged = _sigmoid(gf_ref[...].astype(F32)) * f
    rr = jnp.dot(r_ref[...], wr_ref[...], preferred_element_type=F32)
    merged = merged + _sigmoid(gr_ref[...].astype(F32)) * rr
    c = jnp.dot(cin, wc_ref[...], preferred_element_type=F32)
    merged = merged + _sigmoid(gc_ref[...].astype(F32)) * c
    mix = jnp.dot(merged.astype(BF16), wo_ref[...], preferred_element_type=F32)
    o_ref[...] = _layer_norm(DEEPNORM_ALPHA * x_ref[...] + mix, ln_ref[0:1, :], ln_ref[1:2, :])


def _merge(x, fmix, rin, rest, w_f, w_r, w_c, w_o, conv_w, ln, layer, seq, tm=512):
    T = x.shape[0]
    nb8 = T // 8
    t8 = tm // 8

    def col(cblk):
        return pl.BlockSpec((tm, D_MODEL), lambda i: (i, cblk))

    def prev(cblk):
        return pl.BlockSpec((8, D_MODEL), lambda i: (jnp.maximum(i * t8 - 1, 0), cblk))

    def nxt(cblk):
        return pl.BlockSpec((8, D_MODEL), lambda i: (jnp.minimum((i + 1) * t8, nb8 - 1), cblk))

    def wspec(rows):
        return pl.BlockSpec((None, rows, D_MODEL), lambda i: (layer, 0, 0),
                            pipeline_mode=pl.Buffered(1))

    return pl.pallas_call(
        functools.partial(_merge_kernel, tm=tm, seq=seq),
        grid=(T // tm,),
        in_specs=[col(0), col(0),
                  pl.BlockSpec((tm, RET_V), lambda i: (i, 0)),
                  col(R_CB), col(R_CC), col(R_CX),
                  prev(R_CC), prev(R_CX), nxt(R_CC), nxt(R_CX),
                  col(R_GF), col(R_GR), col(R_GC),
                  wspec(D_MODEL), wspec(RET_V), wspec(D_MODEL), wspec(D_MODEL),
                  pl.BlockSpec((None, 8, D_MODEL), lambda i: (layer, 0, 0)),
                  pl.BlockSpec((None, 8, D_MODEL), lambda i: (layer, 0, 0))],
        out_specs=pl.BlockSpec((tm, D_MODEL), lambda i: (i, 0)),
        out_shape=jax.ShapeDtypeStruct((T, D_MODEL), F32),
        compiler_params=_cparams(1),
        name="merge",
    )(x, fmix, rin, rest, rest, rest, rest, rest, rest, rest, rest, rest, rest,
      w_f, w_r, w_c, w_o, conv_w, ln)


def _ffn_kernel(x_ref, w1_ref, w2_ref, ln_ref, o_ref, ob_ref):
    x = x_ref[...]
    h = jnp.dot(x.astype(BF16), w1_ref[...], preferred_element_type=F32)
    gate = h[:, :FFN_HIDDEN]
    up = h[:, FFN_HIDDEN:]
    a = (gate * _sigmoid(gate) * up).astype(BF16)
    y = jnp.dot(a, w2_ref[...], preferred_element_type=F32)
    out = _layer_norm(DEEPNORM_ALPHA * x + y, ln_ref[0:1, :], ln_ref[1:2, :])
    o_ref[...] = out
    ob_ref[...] = out.astype(BF16)


def _ffn(x, w1, w2, ln, layer, tm=512):
    T = x.shape[0]
    return pl.pallas_call(
        _ffn_kernel,
        grid=(T // tm,),
        in_specs=[pl.BlockSpec((tm, D_MODEL), lambda i: (i, 0)),
                  pl.BlockSpec((None, D_MODEL, 2 * FFN_HIDDEN), lambda i: (layer, 0, 0),
                               pipeline_mode=pl.Buffered(1)),
                  pl.BlockSpec((None, FFN_HIDDEN, D_MODEL), lambda i: (layer, 0, 0),
                               pipeline_mode=pl.Buffered(1)),
                  pl.BlockSpec((None, 8, D_MODEL), lambda i: (layer, 0, 0))],
        out_specs=[pl.BlockSpec((tm, D_MODEL), lambda i: (i, 0)),
                   pl.BlockSpec((tm, D_MODEL), lambda i: (i, 0))],
        out_shape=[jax.ShapeDtypeStruct((T, D_MODEL), F32),
                   jax.ShapeDtypeStruct((T, D_MODEL), BF16)],
        compiler_params=_cparams(1),
        name="ffn",
    )(x, w1, w2, ln)


def _rope_tables(seq):
    inv_freq = 1.0 / (ROPE_BASE ** jnp.linspace(0.0, 1.0, RET_HEAD_QK // 2, dtype=F32))
    ang = jnp.arange(seq, dtype=F32)[:, None] * inv_freq[None, :]
    return jnp.cos(ang), jnp.sin(ang)


def _pad8(a):
    return jnp.pad(a, ((0, 0), (0, 8 - a.shape[1]), (0, 0)))


def _trunk(x, p):
    batch, seq, _ = x.shape
    T = batch * seq
    cos, sin = _rope_tables(seq)
    ftab = _fourier_tables(seq)
    xf = x.reshape(T, D_MODEL)
    xb = xf.astype(BF16)
    for l in range(DEPTH):
        u = _proj(xb, p["w_in"], l, COL_F, D_MODEL, F32)
        qk = _proj_rope(xb, p["w_in"], l, cos, sin, seq)
        rest = _proj(xb, p["w_in"], l, COL_REST, REST_COLS, BF16)
        fmix = _fourier(u, ftab, batch, seq)
        rin = _retention(qk, rest, p["log_gamma"][l], batch, seq)
        x1 = _merge(xf, fmix, rin, rest, p["w_f"], p["w_r"], p["w_c"], p["w_o"],
                    p["conv_w"], p["ln1"], l, seq)
        xf, xb = _ffn(x1, p["w_ffn_in"], p["w_ffn_out"], p["ln2"], l)
    return xf.reshape(batch, seq, D_MODEL)


def kernel(x_prompt, x_sample, w_in, ret_decay_logit, conv_w, w_fourier_out, w_ret_out,
           w_conv_out, w_o, ln_gain, ln_bias, w_ffn_in, w_ffn_out):
    p = {
        "w_in": w_in.astype(BF16),
        "log_gamma": jax.nn.log_sigmoid(ret_decay_logit.astype(F32)),
        "conv_w": _pad8(conv_w.astype(F32)),
        "w_f": w_fourier_out.astype(BF16),
        "w_r": w_ret_out.astype(BF16),
        "w_c": w_conv_out.astype(BF16),
        "w_o": w_o.astype(BF16),
        "ln1": _pad8(jnp.stack([ln_gain[:, 0], ln_bias[:, 0]], axis=1).astype(F32)),
        "ln2": _pad8(jnp.stack([ln_gain[:, 1], ln_bias[:, 1]], axis=1).astype(F32)),
        "w_ffn_in": w_ffn_in.astype(BF16),
        "w_ffn_out": w_ffn_out.astype(BF16),
    }
    return (_trunk(x_prompt, p), _trunk(x_sample, p))
```

```python
import functools
import math

import numpy as np
import jax
import jax.numpy as jnp
from jax import lax
from jax.experimental import pallas as pl
from jax.experimental.pallas import tpu as pltpu

D_MODEL = 1024
DEPTH = 4
FNET_GROUPS = 4
FNET_GROUP_DIM = D_MODEL // FNET_GROUPS
RET_HEAD_QK = 256
RET_HEADS = D_MODEL // RET_HEAD_QK
RET_HEAD_V = 2 * RET_HEAD_QK
RET_QK = RET_HEADS * RET_HEAD_QK
RET_V = RET_HEADS * RET_HEAD_V
ROPE_BASE = 10000.0
FFN_HIDDEN = -(-8 * D_MODEL // (3 * 256)) * 256
DEEPNORM_ALPHA = (2.0 * DEPTH) ** 0.25
LN_EPS = 1e-5
HEAD_NORM_EPS = 1e-6
IN_COLS = 13 * D_MODEL

COL_F = 0
COL_QK = 1
COL_REST = 3
R_V, R_G, R_CB, R_CC, R_CX, R_GF, R_GR, R_GC = 0, 2, 4, 5, 6, 7, 8, 9
REST_COLS = 10 * D_MODEL

MXU_DIM = 256
FFT_N1 = 256
FFT_ROWS = 8
FFT_SLABS_PER_STEP = 8
FFT_UNROLL = 4
FFT_CH_TILES = 4
RET_SUB = 256
RET_UNROLL = 4
VMEM_LIMIT = 56 * 1024 * 1024

BF16 = jnp.bfloat16
F32 = jnp.float32


def _cparams(n_axes, vmem=VMEM_LIMIT):
    return pltpu.CompilerParams(
        dimension_semantics=("arbitrary",) * n_axes, vmem_limit_bytes=vmem)


def _proj_kernel(x_ref, w_ref, o_ref):
    o_ref[...] = jnp.dot(x_ref[...], w_ref[...],
                         preferred_element_type=F32).astype(o_ref.dtype)


def _proj(x, w_in, layer, col0, ncols, out_dtype, tm=2048, tn=1024):
    T = x.shape[0]
    tm = min(tm, T)
    assert (col0 * D_MODEL) % tn == 0 and ncols % tn == 0
    cblk0 = col0 * D_MODEL // tn
    return pl.pallas_call(
        _proj_kernel,
        grid=(ncols // tn, T // tm),
        in_specs=[pl.BlockSpec((tm, D_MODEL), lambda j, i: (i, 0)),
                  pl.BlockSpec((None, D_MODEL, tn), lambda j, i: (layer, 0, cblk0 + j))],
        out_specs=pl.BlockSpec((tm, tn), lambda j, i: (i, j)),
        out_shape=jax.ShapeDtypeStruct((T, ncols), out_dtype),
        compiler_params=_cparams(2),
        name="proj",
    )(x, w_in)


def _proj_rope_kernel(x_ref, w_ref, cos_ref, sin_ref, o_ref):
    j = pl.program_id(0)
    acc = jnp.dot(x_ref[...], w_ref[...], preferred_element_type=F32)
    scale = jnp.where(j == 1, RET_HEAD_QK ** -0.5, 1.0).astype(F32)
    cos = cos_ref[...] * scale
    sin = sin_ref[...] * scale
    half = RET_HEAD_QK // 2
    for h in range(RET_HEADS):
        lo = h * RET_HEAD_QK
        t1 = acc[:, lo:lo + half]
        t2 = acc[:, lo + half:lo + 2 * half]
        o_ref[:, lo:lo + half] = (t1 * cos - t2 * sin).astype(o_ref.dtype)
        o_ref[:, lo + half:lo + 2 * half] = (t1 * sin + t2 * cos).astype(o_ref.dtype)


def _proj_rope(x, w_in, layer, cos, sin, seq, tm=2048):
    T = x.shape[0]
    tn = RET_QK
    tm = min(tm, seq)
    sblk = seq // tm
    return pl.pallas_call(
        _proj_rope_kernel,
        grid=(2, T // tm),
        in_specs=[pl.BlockSpec((tm, D_MODEL), lambda j, i: (i, 0)),
                  pl.BlockSpec((None, D_MODEL, tn), lambda j, i: (layer, 0, COL_QK + j)),
                  pl.BlockSpec((tm, RET_HEAD_QK // 2), lambda j, i: (i % sblk, 0)),
                  pl.BlockSpec((tm, RET_HEAD_QK // 2), lambda j, i: (i % sblk, 0))],
        out_specs=pl.BlockSpec((tm, tn), lambda j, i: (i, j)),
        out_shape=jax.ShapeDtypeStruct((T, 2 * tn), BF16),
        compiler_params=_cparams(2),
        name="proj_rope",
    )(x, w_in, cos, sin)


def _fourier_tables(seq):
    n1n = FFT_N1
    n2n = seq // n1n
    r = FFT_ROWS
    k1 = np.arange(n1n, dtype=np.int64)
    n1 = np.arange(n1n, dtype=np.int64)
    a1 = np.zeros((n2n, 2 * n1n, n1n), np.float64)
    for n2 in range(n2n):
        ang = 2.0 * np.pi * ((k1[:, None] * (n2n * n1[None, :] + n2)) % seq) / seq
        a1[n2, :n1n] = np.cos(ang)
        a1[n2, n1n:] = -np.sin(ang)
    k2 = np.arange(n2n, dtype=np.int64)
    ang2 = 2.0 * np.pi * ((k2[:, None] * k2[None, :]) % n2n) / n2n
    eye = np.eye(r)
    gr = np.kron(np.cos(ang2), eye)
    gi = np.kron(-np.sin(ang2), eye)
    m3 = np.block([[gr, -gi], [gi, gr]])
    cg = np.arange(FNET_GROUP_DIM, dtype=np.int64)
    angg = 2.0 * np.pi * ((cg[:, None] * cg[None, :]) % FNET_GROUP_DIM) / FNET_GROUP_DIM
    a4 = np.concatenate([np.cos(angg), np.sin(angg)], axis=0)
    return (jnp.asarray(a1, BF16), jnp.asarray(m3, BF16), jnp.asarray(a4, BF16))


def _fourier_kernel(*refs, n2n, nps, scale):
    x_refs = refs[:nps]
    a1_ref, m3_ref, a4_ref, o_ref, y_ref, z_ref = refs[nps:]
    n1n, r, g = FFT_N1, FFT_ROWS, FNET_GROUP_DIM
    blk = n2n * r
    t = pl.program_id(2)
    for i in range(nps):
        xs = x_refs[i][...]
        y = jnp.dot(a1_ref[t * nps + i], xs, preferred_element_type=F32)
        y_ref[0, t * nps + i] = y[:n1n]
        y_ref[1, t * nps + i] = y[n1n:]

    @pl.when(t == n2n // nps - 1)
    def _():
        def stage2(j, carry):
            r0 = pl.multiple_of(j * r, r)
            yr = y_ref[0, :, pl.ds(r0, r), :].reshape(blk, g)
            yi = y_ref[1, :, pl.ds(r0, r), :].reshape(blk, g)
            v = jnp.concatenate([yr, yi], axis=0).astype(BF16)
            z = jnp.dot(m3_ref[...], v, preferred_element_type=F32)
            z_ref[j] = jnp.concatenate([z[:blk], z[blk:]], axis=1).astype(BF16)
            return carry

        lax.fori_loop(0, n1n // r, stage2, 0, unroll=FFT_UNROLL)

        def channels(jj, carry):
            zz = z_ref[pl.ds(jj * FFT_CH_TILES, FFT_CH_TILES)].reshape(FFT_CH_TILES * blk, 2 * g)
            o = jnp.dot(zz, a4_ref[...], preferred_element_type=F32) * scale
            o = jnp.concatenate([o[i * blk:(i + 1) * blk].reshape(n2n, r, g)
                                 for i in range(FFT_CH_TILES)], axis=1)
            r0 = pl.multiple_of(jj * (FFT_CH_TILES * r), FFT_CH_TILES * r)
            o_ref[:, pl.ds(r0, FFT_CH_TILES * r), :] = o.astype(o_ref.dtype)
            return carry

        lax.fori_loop(0, n1n // r // FFT_CH_TILES, channels, 0, unroll=2)


def _fourier(u, tables, batch, seq):
    a1, m3, a4 = tables
    n1n, r, g = FFT_N1, FFT_ROWS, FNET_GROUP_DIM
    n2n = seq // n1n
    nps = min(FFT_SLABS_PER_STEP, n2n)
    blk = n2n * r
    kern = functools.partial(_fourier_kernel, n2n=n2n, nps=nps,
                             scale=1.0 / math.sqrt(seq * g))

    def xspec(i):
        return pl.BlockSpec((None, n1n, g),
                            lambda b, c, t: (b, 0, (t * nps + i) * FNET_GROUPS + c))

    uv = u.reshape(batch, n1n, n2n * D_MODEL)
    out = pl.pallas_call(
        kern,
        grid=(batch, FNET_GROUPS, n2n // nps),
        in_specs=[xspec(i) for i in range(nps)] + [
            pl.BlockSpec((n2n, 2 * n1n, n1n), lambda b, c, t: (0, 0, 0),
                         pipeline_mode=pl.Buffered(1)),
            pl.BlockSpec((2 * blk, 2 * blk), lambda b, c, t: (0, 0)),
            pl.BlockSpec((2 * g, g), lambda b, c, t: (0, 0))],
        out_specs=pl.BlockSpec((None, n2n, n1n, g), lambda b, c, t: (b, 0, 0, c)),
        out_shape=jax.ShapeDtypeStruct((batch, n2n, n1n, D_MODEL), BF16),
        scratch_shapes=[pltpu.VMEM((2, n2n, n1n, g), F32),
                        pltpu.VMEM((n1n // r, blk, 2 * g), BF16)],
        compiler_params=_cparams(3),
        name="fourier",
    )(*([uv] * nps), a1, m3, a4)
    return out.reshape(batch * seq, D_MODEL)


def _ret_kernel(lg_ref, q_ref, k_ref, v_ref, g_ref, ka_ref, va_ref, o_ref,
                sb_ref, stf_ref, stb_ref, dec_ref, dmat_ref, oa_ref, ob_ref, pa_ref, pb_ref,
                *, nb, nsub, n_items):
    C = RET_SUB
    H = RET_HEADS
    i = pl.program_id(0)
    t = pl.program_id(1)
    h_rev = i % H
    h_fwd = (i + H - 1) % H
    lgf = lg_ref[0, h_fwd]
    lgb = lg_ref[1, h_fwd]
    lgb_rev = lg_ref[1, h_rev]

    @pl.when(t == 0)
    def _():
        ii = lax.broadcasted_iota(jnp.int32, (C, RET_HEAD_QK), 0).astype(F32)
        dec_ref[0] = jnp.exp((ii + 1.0) * lgf).astype(BF16)
        dec_ref[1] = jnp.exp((C - ii) * lgb).astype(BF16)
        dec_ref[2] = jnp.exp((C - 1.0 - ii) * lgf).astype(BF16)
        dec_ref[3] = jnp.exp(ii * lgb_rev).astype(BF16)
        ri = lax.broadcasted_iota(jnp.int32, (C, C), 0)
        ci = lax.broadcasted_iota(jnp.int32, (C, C), 1)
        dif = (ri - ci).astype(F32)
        dmat_ref[...] = jnp.where(ri >= ci, jnp.exp(jnp.maximum(dif, 0.0) * lgf),
                                  jnp.exp(jnp.maximum(-dif, 0.0) * lgb))
        stf_ref[...] = jnp.zeros_like(stf_ref)
        stb_ref[...] = jnp.zeros_like(stb_ref)

    def rows_of(c):
        return pl.ds(pl.multiple_of(c * C, C), C)

    def kv_update(st_ref, kc, vc, dk, chunk_decay):
        upd = lax.dot_general(kc * dk, vc, (((0,), (0,)), ((), ())),
                              preferred_element_type=F32)
        st_ref[...] = st_ref[...] * chunk_decay + upd

    @pl.when(i < n_items)
    def _():
        slot = i % 2
        blk = nb - 1 - t
        decay = jnp.exp(jnp.full((1, RET_HEAD_V), C * lgb_rev, F32))

        def body(s, carry):
            c = nsub - 1 - s
            rows = rows_of(c)
            sb_ref[slot, blk * nsub + c] = stb_ref[...].astype(BF16)
            kv_update(stb_ref, ka_ref[rows, :], va_ref[rows, :], dec_ref[3], decay)
            return carry

        lax.fori_loop(0, nsub, body, 0, unroll=RET_UNROLL)

    @pl.when(i > 0)
    def _():
        slot = (i + 1) % 2
        blk = t
        decay = jnp.exp(jnp.full((1, RET_HEAD_V), C * lgf, F32))

        def scores(c, pbuf):
            rows = rows_of(c)
            s = lax.dot_general(q_ref[rows, :], k_ref[rows, :], (((1,), (1,)), ((), ())),
                                preferred_element_type=F32)
            pbuf[...] = (s * dmat_ref[...]).astype(BF16)

        def mix(c, pbuf, obuf):
            rows = rows_of(c)
            qc = q_ref[rows, :]
            vc = v_ref[rows, :]
            lhs = jnp.concatenate([pbuf[...], qc * dec_ref[0], qc * dec_ref[1]], axis=1)
            rhs = jnp.concatenate([vc, stf_ref[...].astype(BF16),
                                   sb_ref[slot, blk * nsub + c]], axis=0)
            obuf[...] = jnp.dot(lhs, rhs, preferred_element_type=F32)
            kv_update(stf_ref, k_ref[rows, :], vc, dec_ref[2], decay)

        def norm_gate(c, obuf):
            rows = rows_of(c)
            o = obuf[...]
            mu = jnp.mean(o, axis=-1, keepdims=True)
            oc = o - mu
            on = oc * lax.rsqrt(jnp.mean(oc * oc, axis=-1, keepdims=True) + HEAD_NORM_EPS)
            gg = g_ref[rows, :]
            silu = gg * (1.0 / (1.0 + jnp.exp(-gg)))
            o_ref[rows, :] = (on.astype(BF16) * silu).astype(o_ref.dtype)

        scores(0, pa_ref)
        scores(1, pb_ref)
        mix(0, pa_ref, oa_ref)

        def body(m, carry):
            c = 2 * m + 1
            scores(c + 1, pa_ref)
            mix(c, pb_ref, ob_ref)
            norm_gate(c - 1, oa_ref)
            scores(c + 2, pb_ref)
            mix(c + 1, pa_ref, oa_ref)
            norm_gate(c, ob_ref)
            return carry

        lax.fori_loop(0, nsub // 2 - 1, body, 0)
        mix(nsub - 1, pb_ref, ob_ref)
        norm_gate(nsub - 2, oa_ref)
        norm_gate(nsub - 1, ob_ref)


def _retention(qk, rest, log_gamma, batch, seq, cb=2048):
    cb = min(cb, seq)
    nb = seq // cb
    nsub = cb // RET_SUB
    T = batch * seq
    H = RET_HEADS
    n_items = batch * H
    vcol = (R_V * D_MODEL) // RET_HEAD_V
    gcol = (R_G * D_MODEL) // RET_HEAD_V

    def fwd_row(i, t):
        it = jnp.maximum(i - 1, 0)
        return (it // H) * nb + t, it % H

    def rev_row(i, t):
        it = jnp.minimum(i, n_items - 1)
        return (it // H) * nb + (nb - 1 - t), it % H

    def qmap(i, t, lg):
        row, h = fwd_row(i, t)
        return (row, h)

    def kmap(i, t, lg):
        row, h = fwd_row(i, t)
        return (row, H + h)

    def vmap_(i, t, lg):
        row, h = fwd_row(i, t)
        return (row, vcol + h)

    def gmap(i, t, lg):
        row, h = fwd_row(i, t)
        return (row, gcol + h)

    def kamap(i, t, lg):
        row, h = rev_row(i, t)
        return (row, H + h)

    def vamap(i, t, lg):
        row, h = rev_row(i, t)
        return (row, vcol + h)

    def omap(i, t, lg):
        row, h = fwd_row(i, jnp.where(i == 0, 0, t))
        return (row, h)

    grid_spec = pltpu.PrefetchScalarGridSpec(
        num_scalar_prefetch=1,
        grid=(n_items + 1, nb),
        in_specs=[pl.BlockSpec((cb, RET_HEAD_QK), qmap),
                  pl.BlockSpec((cb, RET_HEAD_QK), kmap),
                  pl.BlockSpec((cb, RET_HEAD_V), vmap_),
                  pl.BlockSpec((cb, RET_HEAD_V), gmap),
                  pl.BlockSpec((cb, RET_HEAD_QK), kamap),
                  pl.BlockSpec((cb, RET_HEAD_V), vamap)],
        out_specs=pl.BlockSpec((cb, RET_HEAD_V), omap),
        scratch_shapes=[pltpu.VMEM((2, seq // RET_SUB, RET_HEAD_QK, RET_HEAD_V), BF16),
                        pltpu.VMEM((RET_HEAD_QK, RET_HEAD_V), F32),
                        pltpu.VMEM((RET_HEAD_QK, RET_HEAD_V), F32),
                        pltpu.VMEM((4, RET_SUB, RET_HEAD_QK), BF16),
                        pltpu.VMEM((RET_SUB, RET_SUB), F32),
                        pltpu.VMEM((RET_SUB, RET_HEAD_V), F32),
                        pltpu.VMEM((RET_SUB, RET_HEAD_V), F32),
                        pltpu.VMEM((RET_SUB, RET_SUB), BF16),
                        pltpu.VMEM((RET_SUB, RET_SUB), BF16)],
    )
    return pl.pallas_call(
        functools.partial(_ret_kernel, nb=nb, nsub=nsub, n_items=n_items),
        grid_spec=grid_spec,
        out_shape=jax.ShapeDtypeStruct((T, RET_V), BF16),
        compiler_params=_cparams(2),
        name="retention",
    )(log_gamma, qk, qk, rest, rest, qk, rest)


def _layer_norm(y, gain, bias):
    mu = jnp.mean(y, axis=-1, keepdims=True)
    yc = y - mu
    var = jnp.mean(yc * yc, axis=-1, keepdims=True)
    return yc * lax.rsqrt(var + LN_EPS) * gain + bias


def _sigmoid(z):
    return 1.0 / (1.0 + jnp.exp(-z))


def _merge_kernel(x_ref, f_ref, r_ref, cb_ref, cc_ref, cx_ref,
                  ccp_ref, cxp_ref, ccn_ref, cxn_ref,
                  gf_ref, gr_ref, gc_ref,
                  wf_ref, wr_ref, wc_ref, wo_ref, cw_ref, ln_ref,
                  o_ref, *, tm, seq):
    i = pl.program_id(0)
    pos0 = (i * tm) % seq
    u = cc_ref[...].astype(F32) * cx_ref[...].astype(F32)
    keep_p = jnp.where(pos0 == 0, 0.0, 1.0).astype(F32)
    keep_n = jnp.where(pos0 + tm == seq, 0.0, 1.0).astype(F32)
    up_row = ccp_ref[7:8, :].astype(F32) * cxp_ref[7:8, :].astype(F32) * keep_p
    un_row = ccn_ref[0:1, :].astype(F32) * cxn_ref[0:1, :].astype(F32) * keep_n
    rows = lax.broadcasted_iota(jnp.int32, u.shape, 0)
    u_prev = jnp.where(rows == 0, up_row, pltpu.roll(u, 1, 0))
    u_next = jnp.where(rows == tm - 1, un_row, pltpu.roll(u, tm - 1, 0))
    y = cw_ref[0:1, :] * u_prev + cw_ref[1:2, :] * u + cw_ref[2:3, :] * u_next
    cin = (cb_ref[...].astype(F32) * y).astype(BF16)

    f = jnp.dot(f_ref[...], wf_ref[...], preferred_element_type=F32)
    merged = _sigmoid(gf_ref[...].astype(F32)) * f
    rr = jnp.dot(r_ref[...], wr_ref[...], preferred_element_type=F32)
    merged = merged + _sigmoid(gr_ref[...].astype(F32)) * rr
    c = jnp.dot(cin, wc_ref[...], preferred_element_type=F32)
    merged = merged + _sigmoid(gc_ref[...].astype(F32)) * c
    mix = jnp.dot(merged.astype(BF16), wo_ref[...], preferred_element_type=F32)
    o_ref[...] = _layer_norm(DEEPNORM_ALPHA * x_ref[...] + mix, ln_ref[0:1, :], ln_ref[1:2, :])


def _merge(x, fmix, rin, rest, w_f, w_r, w_c, w_o, conv_w, ln, layer, seq, tm=512):
    T = x.shape[0]
    nb8 = T // 8
    t8 = tm // 8

    def col(cblk):
        return pl.BlockSpec((tm, D_MODEL), lambda i: (i, cblk))

    def prev(cblk):
        return pl.BlockSpec((8, D_MODEL), lambda i: (jnp.maximum(i * t8 - 1, 0), cblk))

    def nxt(cblk):
        return pl.BlockSpec((8, D_MODEL), lambda i: (jnp.minimum((i + 1) * t8, nb8 - 1), cblk))

    def wspec(rows):
        return pl.BlockSpec((None, rows, D_MODEL), lambda i: (layer, 0, 0),
                            pipeline_mode=pl.Buffered(1))

    return pl.pallas_call(
        functools.partial(_merge_kernel, tm=tm, seq=seq),
        grid=(T // tm,),
        in_specs=[col(0), col(0),
                  pl.BlockSpec((tm, RET_V), lambda i: (i, 0)),
                  col(R_CB), col(R_CC), col(R_CX),
                  prev(R_CC), prev(R_CX), nxt(R_CC), nxt(R_CX),
                  col(R_GF), col(R_GR), col(R_GC),
                  wspec(D_MODEL), wspec(RET_V), wspec(D_MODEL), wspec(D_MODEL),
                  pl.BlockSpec((None, 8, D_MODEL), lambda i: (layer, 0, 0)),
                  pl.BlockSpec((None, 8, D_MODEL), lambda i: (layer, 0, 0))],
        out_specs=pl.BlockSpec((tm, D_MODEL), lambda i: (i, 0)),
        out_shape=jax.ShapeDtypeStruct((T, D_MODEL), F32),
        compiler_params=_cparams(1),
        name="merge",
    )(x, fmix, rin, rest, rest, rest, rest, rest, rest, rest, rest, rest, rest,
      w_f, w_r, w_c, w_o, conv_w, ln)


def _ffn_kernel(x_ref, w1_ref, w2_ref, ln_ref, o_ref, ob_ref):
    x = x_ref[...]
    h = jnp.dot(x.astype(BF16), w1_ref[...], preferred_element_type=F32)
    gate = h[:, :FFN_HIDDEN]
    up = h[:, FFN_HIDDEN:]
    a = (gate * _sigmoid(gate) * up).astype(BF16)
    y = jnp.dot(a, w2_ref[...], preferred_element_type=F32)
    out = _layer_norm(DEEPNORM_ALPHA * x + y, ln_ref[0:1, :], ln_ref[1:2, :])
    o_ref[...] = out
    ob_ref[...] = out.astype(BF16)


def _ffn(x, w1, w2, ln, layer, tm=512):
    T = x.shape[0]
    return pl.pallas_call(
        _ffn_kernel,
        grid=(T // tm,),
        in_specs=[pl.BlockSpec((tm, D_MODEL), lambda i: (i, 0)),
                  pl.BlockSpec((None, D_MODEL, 2 * FFN_HIDDEN), lambda i: (layer, 0, 0),
                               pipeline_mode=pl.Buffered(1)),
                  pl.BlockSpec((None, FFN_HIDDEN, D_MODEL), lambda i: (layer, 0, 0),
                               pipeline_mode=pl.Buffered(1)),
                  pl.BlockSpec((None, 8, D_MODEL), lambda i: (layer, 0, 0))],
        out_specs=[pl.BlockSpec((tm, D_MODEL), lambda i: (i, 0)),
                   pl.BlockSpec((tm, D_MODEL), lambda i: (i, 0))],
        out_shape=[jax.ShapeDtypeStruct((T, D_MODEL), F32),
                   jax.ShapeDtypeStruct((T, D_MODEL), BF16)],
        compiler_params=_cparams(1),
        name="ffn",
    )(x, w1, w2, ln)


def _rope_tables(seq):
    inv_freq = 1.0 / (ROPE_BASE ** jnp.linspace(0.0, 1.0, RET_HEAD_QK // 2, dtype=F32))
    ang = jnp.arange(seq, dtype=F32)[:, None] * inv_freq[None, :]
    return jnp.cos(ang), jnp.sin(ang)


def _pad8(a):
    return jnp.pad(a, ((0, 0), (0, 8 - a.shape[1]), (0, 0)))


def _trunk(x, p):
    batch, seq, _ = x.shape
    T = batch * seq
    cos, sin = _rope_tables(seq)
    ftab = _fourier_tables(seq)
    xf = x.reshape(T, D_MODEL)
    xb = xf.astype(BF16)
    for l in range(DEPTH):
        u = _proj(xb, p["w_in"], l, COL_F, D_MODEL, BF16)
        qk = _proj_rope(xb, p["w_in"], l, cos, sin, seq)
        rest = _proj(xb, p["w_in"], l, COL_REST, REST_COLS, BF16)
        fmix = _fourier(u, ftab, batch, seq)
        rin = _retention(qk, rest, p["log_gamma"][l], batch, seq)
        x1 = _merge(xf, fmix, rin, rest, p["w_f"], p["w_r"], p["w_c"], p["w_o"],
                    p["conv_w"], p["ln1"], l, seq)
        xf, xb = _ffn(x1, p["w_ffn_in"], p["w_ffn_out"], p["ln2"], l)
    return xf.reshape(batch, seq, D_MODEL)


def kernel(x_prompt, x_sample, w_in, ret_decay_logit, conv_w, w_fourier_out, w_ret_out,
           w_conv_out, w_o, ln_gain, ln_bias, w_ffn_in, w_ffn_out):
    p = {
        "w_in": w_in.astype(BF16),
        "log_gamma": jax.nn.log_sigmoid(ret_decay_logit.astype(F32)),
        "conv_w": _pad8(conv_w.astype(F32)),
        "w_f": w_fourier_out.astype(BF16),
        "w_r": w_ret_out.astype(BF16),
        "w_c": w_conv_out.astype(BF16),
        "w_o": w_o.astype(BF16),
        "ln1": _pad8(jnp.stack([ln_gain[:, 0], ln_bias[:, 0]], axis=1).astype(F32)),
        "ln2": _pad8(jnp.stack([ln_gain[:, 1], ln_bias[:, 1]], axis=1).astype(F32)),
        "w_ffn_in": w_ffn_in.astype(BF16),
        "w_ffn_out": w_ffn_out.astype(BF16),
    }
    return (_trunk(x_prompt, p), _trunk(x_sample, p))
```

```python
import functools
import math

import numpy as np
import jax
import jax.numpy as jnp
from jax import lax
from jax.experimental import pallas as pl
from jax.experimental.pallas import tpu as pltpu

D_MODEL = 1024
DEPTH = 4
FNET_GROUPS = 4
FNET_GROUP_DIM = D_MODEL // FNET_GROUPS
RET_HEAD_QK = 256
RET_HEADS = D_MODEL // RET_HEAD_QK
RET_HEAD_V = 2 * RET_HEAD_QK
RET_QK = RET_HEADS * RET_HEAD_QK
RET_V = RET_HEADS * RET_HEAD_V
ROPE_BASE = 10000.0
FFN_HIDDEN = -(-8 * D_MODEL // (3 * 256)) * 256
DEEPNORM_ALPHA = (2.0 * DEPTH) ** 0.25
LN_EPS = 1e-5
HEAD_NORM_EPS = 1e-6
IN_COLS = 13 * D_MODEL

COL_F = 0
COL_QK = 1
COL_REST = 3
R_V, R_G, R_CB, R_CC, R_CX, R_GF, R_GR, R_GC = 0, 2, 4, 5, 6, 7, 8, 9
REST_COLS = 10 * D_MODEL

MXU_DIM = 256
FFT_N1 = 256
FFT_ROWS = 8
FFT_SLABS_PER_STEP = 32
FFT_UNROLL = 4
FFT_CH_TILES = 4
RET_SUB = 256
RET_UNROLL = 4
ROPE_SLAB = 512
FFN_SLAB = 256
VMEM_LIMIT = 56 * 1024 * 1024

BF16 = jnp.bfloat16
F32 = jnp.float32


def _cparams(n_axes, vmem=VMEM_LIMIT):
    return pltpu.CompilerParams(
        dimension_semantics=("arbitrary",) * n_axes, vmem_limit_bytes=vmem)


def _proj_kernel(x_ref, w_ref, o_ref):
    o_ref[...] = jnp.dot(x_ref[...], w_ref[...],
                         preferred_element_type=F32).astype(o_ref.dtype)


def _proj(x, w_in, layer, col0, ncols, out_dtype, tm=2048, tn=1024):
    T = x.shape[0]
    tm = min(tm, T)
    assert (col0 * D_MODEL) % tn == 0 and ncols % tn == 0
    cblk0 = col0 * D_MODEL // tn
    return pl.pallas_call(
        _proj_kernel,
        grid=(ncols // tn, T // tm),
        in_specs=[pl.BlockSpec((tm, D_MODEL), lambda j, i: (i, 0)),
                  pl.BlockSpec((None, D_MODEL, tn), lambda j, i: (layer, 0, cblk0 + j))],
        out_specs=pl.BlockSpec((tm, tn), lambda j, i: (i, j)),
        out_shape=jax.ShapeDtypeStruct((T, ncols), out_dtype),
        compiler_params=_cparams(2),
        name="proj",
    )(x, w_in)


def _proj_rope_kernel(x_ref, w_ref, cos_ref, sin_ref, o_ref):
    j = pl.program_id(0)
    scale = jnp.where(j == 1, RET_HEAD_QK ** -0.5, 1.0).astype(F32)
    half = RET_HEAD_QK // 2
    tm = x_ref.shape[0]
    ns = max(tm // ROPE_SLAB, 1)
    hs = tm // ns
    for sl in range(ns):
        rs = slice(sl * hs, (sl + 1) * hs)
        acc = jnp.dot(x_ref[rs, :], w_ref[...], preferred_element_type=F32)
        cos = cos_ref[rs, :] * scale
        sin = sin_ref[rs, :] * scale
        for h in range(RET_HEADS):
            lo = h * RET_HEAD_QK
            t1 = acc[:, lo:lo + half]
            t2 = acc[:, lo + half:lo + 2 * half]
            o_ref[rs, lo:lo + half] = (t1 * cos - t2 * sin).astype(o_ref.dtype)
            o_ref[rs, lo + half:lo + 2 * half] = (t1 * sin + t2 * cos).astype(o_ref.dtype)


def _proj_rope(x, w_in, layer, cos, sin, seq, tm=2048):
    T = x.shape[0]
    tn = RET_QK
    tm = min(tm, seq)
    sblk = seq // tm
    return pl.pallas_call(
        _proj_rope_kernel,
        grid=(2, T // tm),
        in_specs=[pl.BlockSpec((tm, D_MODEL), lambda j, i: (i, 0)),
                  pl.BlockSpec((None, D_MODEL, tn), lambda j, i: (layer, 0, COL_QK + j)),
                  pl.BlockSpec((tm, RET_HEAD_QK // 2), lambda j, i: (i % sblk, 0)),
                  pl.BlockSpec((tm, RET_HEAD_QK // 2), lambda j, i: (i % sblk, 0))],
        out_specs=pl.BlockSpec((tm, tn), lambda j, i: (i, j)),
        out_shape=jax.ShapeDtypeStruct((T, 2 * tn), BF16),
        compiler_params=_cparams(2),
        name="proj_rope",
    )(x, w_in, cos, sin)


def _fourier_tables(seq):
    n1n = FFT_N1
    n2n = seq // n1n
    r = FFT_ROWS
    k1 = np.arange(n1n, dtype=np.int64)
    n1 = np.arange(n1n, dtype=np.int64)
    a1 = np.zeros((n2n, 2 * n1n, n1n), np.float64)
    for n2 in range(n2n):
        ang = 2.0 * np.pi * ((k1[:, None] * (n2n * n1[None, :] + n2)) % seq) / seq
        a1[n2, :n1n] = np.cos(ang)
        a1[n2, n1n:] = -np.sin(ang)
    k2 = np.arange(n2n, dtype=np.int64)
    ang2 = 2.0 * np.pi * ((k2[:, None] * k2[None, :]) % n2n) / n2n
    eye = np.eye(r)
    gr = np.kron(np.cos(ang2), eye)
    gi = np.kron(-np.sin(ang2), eye)
    m3 = np.block([[gr, -gi], [gi, gr]])
    cg = np.arange(FNET_GROUP_DIM, dtype=np.int64)
    angg = 2.0 * np.pi * ((cg[:, None] * cg[None, :]) % FNET_GROUP_DIM) / FNET_GROUP_DIM
    a4 = np.concatenate([np.cos(angg), np.sin(angg)], axis=0)
    return (jnp.asarray(a1, BF16), jnp.asarray(m3, BF16), jnp.asarray(a4, BF16))


def _fourier_kernel(*refs, n2n, nps, scale):
    x_refs = refs[:nps]
    a1_ref, m3_ref, a4_ref, o_ref, y_ref, z_ref = refs[nps:]
    n1n, r, g = FFT_N1, FFT_ROWS, FNET_GROUP_DIM
    blk = n2n * r
    t = pl.program_id(2)
    for i in range(nps):
        xs = x_refs[i][...]
        y = jnp.dot(a1_ref[t * nps + i], xs, preferred_element_type=F32)
        y_ref[0, t * nps + i] = y[:n1n]
        y_ref[1, t * nps + i] = y[n1n:]

    @pl.when(t == n2n // nps - 1)
    def _():
        def stage2(j, carry):
            r0 = pl.multiple_of(j * r, r)
            yr = y_ref[0, :, pl.ds(r0, r), :].reshape(blk, g)
            yi = y_ref[1, :, pl.ds(r0, r), :].reshape(blk, g)
            v = jnp.concatenate([yr, yi], axis=0).astype(BF16)
            z = jnp.dot(m3_ref[...], v, preferred_element_type=F32)
            z_ref[j] = jnp.concatenate([z[:blk], z[blk:]], axis=1).astype(BF16)
            return carry

        lax.fori_loop(0, n1n // r, stage2, 0, unroll=FFT_UNROLL)

        def channels(jj, carry):
            zz = z_ref[pl.ds(jj * FFT_CH_TILES, FFT_CH_TILES)].reshape(FFT_CH_TILES * blk, 2 * g)
            o = jnp.dot(zz, a4_ref[...], preferred_element_type=F32) * scale
            o = jnp.concatenate([o[i * blk:(i + 1) * blk].reshape(n2n, r, g)
                                 for i in range(FFT_CH_TILES)], axis=1)
            r0 = pl.multiple_of(jj * (FFT_CH_TILES * r), FFT_CH_TILES * r)
            o_ref[:, pl.ds(r0, FFT_CH_TILES * r), :] = o.astype(o_ref.dtype)
            return carry

        lax.fori_loop(0, n1n // r // FFT_CH_TILES, channels, 0, unroll=2)


def _fourier(u, tables, batch, seq):
    a1, m3, a4 = tables
    n1n, r, g = FFT_N1, FFT_ROWS, FNET_GROUP_DIM
    n2n = seq // n1n
    nps = min(FFT_SLABS_PER_STEP, n2n)
    blk = n2n * r
    kern = functools.partial(_fourier_kernel, n2n=n2n, nps=nps,
                             scale=1.0 / math.sqrt(seq * g))

    def xspec(i):
        return pl.BlockSpec((None, n1n, g),
                            lambda b, c, t: (b, 0, (t * nps + i) * FNET_GROUPS + c))

    uv = u.reshape(batch, n1n, n2n * D_MODEL)
    out = pl.pallas_call(
        kern,
        grid=(batch, FNET_GROUPS, n2n // nps),
        in_specs=[xspec(i) for i in range(nps)] + [
            pl.BlockSpec((n2n, 2 * n1n, n1n), lambda b, c, t: (0, 0, 0),
                         pipeline_mode=pl.Buffered(1)),
            pl.BlockSpec((2 * blk, 2 * blk), lambda b, c, t: (0, 0)),
            pl.BlockSpec((2 * g, g), lambda b, c, t: (0, 0))],
        out_specs=pl.BlockSpec((None, n2n, n1n, g), lambda b, c, t: (b, 0, 0, c)),
        out_shape=jax.ShapeDtypeStruct((batch, n2n, n1n, D_MODEL), BF16),
        scratch_shapes=[pltpu.VMEM((2, n2n, n1n, g), F32),
                        pltpu.VMEM((n1n // r, blk, 2 * g), BF16)],
        compiler_params=_cparams(3),
        name="fourier",
    )(*([uv] * nps), a1, m3, a4)
    return out.reshape(batch * seq, D_MODEL)


def _ret_kernel(lg_ref, q_ref, k_ref, v_ref, g_ref, ka_ref, va_ref, o_ref,
                sb_ref, stf_ref, stb_ref, dec_ref, dmat_ref, oa_ref, ob_ref, pa_ref, pb_ref,
                *, nb, nsub, n_items):
    C = RET_SUB
    H = RET_HEADS
    i = pl.program_id(0)
    t = pl.program_id(1)
    h_rev = i % H
    h_fwd = (i + H - 1) % H
    lgf = lg_ref[0, h_fwd]
    lgb = lg_ref[1, h_fwd]
    lgb_rev = lg_ref[1, h_rev]

    @pl.when(t == 0)
    def _():
        ii = lax.broadcasted_iota(jnp.int32, (C, RET_HEAD_QK), 0).astype(F32)
        dec_ref[0] = jnp.exp((ii + 1.0) * lgf).astype(BF16)
        dec_ref[1] = jnp.exp((C - ii) * lgb).astype(BF16)
        dec_ref[2] = jnp.exp((C - 1.0 - ii) * lgf).astype(BF16)
        dec_ref[3] = jnp.exp(ii * lgb_rev).astype(BF16)
        ri = lax.broadcasted_iota(jnp.int32, (C, C), 0)
        ci = lax.broadcasted_iota(jnp.int32, (C, C), 1)
        dif = (ri - ci).astype(F32)
        dmat_ref[...] = jnp.where(ri >= ci, jnp.exp(jnp.maximum(dif, 0.0) * lgf),
                                  jnp.exp(jnp.maximum(-dif, 0.0) * lgb))
        stf_ref[...] = jnp.zeros_like(stf_ref)
        stb_ref[...] = jnp.zeros_like(stb_ref)

    def rows_of(c):
        return pl.ds(pl.multiple_of(c * C, C), C)

    def kv_update(st_ref, kc, vc, dk, chunk_decay):
        upd = lax.dot_general(kc * dk, vc, (((0,), (0,)), ((), ())),
                              preferred_element_type=F32)
        st_ref[...] = st_ref[...] * chunk_decay + upd

    @pl.when(i < n_items)
    def _():
        slot = i % 2
        blk = nb - 1 - t
        decay = jnp.exp(jnp.full((1, RET_HEAD_V), C * lgb_rev, F32))

        def body(s, carry):
            c = nsub - 1 - s
            rows = rows_of(c)
            sb_ref[slot, blk * nsub + c] = stb_ref[...].astype(BF16)
            kv_update(stb_ref, ka_ref[rows, :], va_ref[rows, :], dec_ref[3], decay)
            return carry

        lax.fori_loop(0, nsub, body, 0, unroll=RET_UNROLL)

    @pl.when(i > 0)
    def _():
        slot = (i + 1) % 2
        blk = t
        decay = jnp.exp(jnp.full((1, RET_HEAD_V), C * lgf, F32))

        def scores(c, pbuf):
            rows = rows_of(c)
            s = lax.dot_general(q_ref[rows, :], k_ref[rows, :], (((1,), (1,)), ((), ())),
                                preferred_element_type=F32)
            pbuf[...] = (s * dmat_ref[...]).astype(BF16)

        def mix(c, pbuf, obuf):
            rows = rows_of(c)
            qc = q_ref[rows, :]
            vc = v_ref[rows, :]
            lhs = jnp.concatenate([pbuf[...], qc * dec_ref[0], qc * dec_ref[1]], axis=1)
            rhs = jnp.concatenate([vc, stf_ref[...].astype(BF16),
                                   sb_ref[slot, blk * nsub + c]], axis=0)
            obuf[...] = jnp.dot(lhs, rhs, preferred_element_type=F32)
            kv_update(stf_ref, k_ref[rows, :], vc, dec_ref[2], decay)

        def norm_gate(c, obuf):
            rows = rows_of(c)
            o = obuf[...]
            mu = jnp.mean(o, axis=-1, keepdims=True)
            oc = o - mu
            on = oc * lax.rsqrt(jnp.mean(oc * oc, axis=-1, keepdims=True) + HEAD_NORM_EPS)
            gg = g_ref[rows, :]
            silu = gg * (1.0 / (1.0 + jnp.exp(-gg)))
            o_ref[rows, :] = (on.astype(BF16) * silu).astype(o_ref.dtype)

        scores(0, pa_ref)
        scores(1, pb_ref)
        mix(0, pa_ref, oa_ref)

        def body(m, carry):
            c = 2 * m + 1
            scores(c + 1, pa_ref)
            mix(c, pb_ref, ob_ref)
            norm_gate(c - 1, oa_ref)
            scores(c + 2, pb_ref)
            mix(c + 1, pa_ref, oa_ref)
            norm_gate(c, ob_ref)
            return carry

        lax.fori_loop(0, nsub // 2 - 1, body, 0)
        mix(nsub - 1, pb_ref, ob_ref)
        norm_gate(nsub - 2, oa_ref)
        norm_gate(nsub - 1, ob_ref)


def _retention(qk, rest, log_gamma, batch, seq, cb=2048):
    cb = min(cb, seq)
    nb = seq // cb
    nsub = cb // RET_SUB
    T = batch * seq
    H = RET_HEADS
    n_items = batch * H
    vcol = (R_V * D_MODEL) // RET_HEAD_V
    gcol = (R_G * D_MODEL) // RET_HEAD_V

    def fwd_row(i, t):
        it = jnp.maximum(i - 1, 0)
        return (it // H) * nb + t, it % H

    def rev_row(i, t):
        it = jnp.minimum(i, n_items - 1)
        return (it // H) * nb + (nb - 1 - t), it % H

    def qmap(i, t, lg):
        row, h = fwd_row(i, t)
        return (row, h)

    def kmap(i, t, lg):
        row, h = fwd_row(i, t)
        return (row, H + h)

    def vmap_(i, t, lg):
        row, h = fwd_row(i, t)
        return (row, vcol + h)

    def gmap(i, t, lg):
        row, h = fwd_row(i, t)
        return (row, gcol + h)

    def kamap(i, t, lg):
        row, h = rev_row(i, t)
        return (row, H + h)

    def vamap(i, t, lg):
        row, h = rev_row(i, t)
        return (row, vcol + h)

    def omap(i, t, lg):
        row, h = fwd_row(i, jnp.where(i == 0, 0, t))
        return (row, h)

    grid_spec = pltpu.PrefetchScalarGridSpec(
        num_scalar_prefetch=1,
        grid=(n_items + 1, nb),
        in_specs=[pl.BlockSpec((cb, RET_HEAD_QK), qmap),
                  pl.BlockSpec((cb, RET_HEAD_QK), kmap),
                  pl.BlockSpec((cb, RET_HEAD_V), vmap_),
                  pl.BlockSpec((cb, RET_HEAD_V), gmap),
                  pl.BlockSpec((cb, RET_HEAD_QK), kamap),
                  pl.BlockSpec((cb, RET_HEAD_V), vamap)],
        out_specs=pl.BlockSpec((cb, RET_HEAD_V), omap),
        scratch_shapes=[pltpu.VMEM((2, seq // RET_SUB, RET_HEAD_QK, RET_HEAD_V), BF16),
                        pltpu.VMEM((RET_HEAD_QK, RET_HEAD_V), F32),
                        pltpu.VMEM((RET_HEAD_QK, RET_HEAD_V), F32),
                        pltpu.VMEM((4, RET_SUB, RET_HEAD_QK), BF16),
                        pltpu.VMEM((RET_SUB, RET_SUB), F32),
                        pltpu.VMEM((RET_SUB, RET_HEAD_V), F32),
                        pltpu.VMEM((RET_SUB, RET_HEAD_V), F32),
                        pltpu.VMEM((RET_SUB, RET_SUB), BF16),
                        pltpu.VMEM((RET_SUB, RET_SUB), BF16)],
    )
    return pl.pallas_call(
        functools.partial(_ret_kernel, nb=nb, nsub=nsub, n_items=n_items),
        grid_spec=grid_spec,
        out_shape=jax.ShapeDtypeStruct((T, RET_V), BF16),
        compiler_params=_cparams(2),
        name="retention",
    )(log_gamma, qk, qk, rest, rest, qk, rest)


def _layer_norm(y, gain, bias):
    mu = jnp.mean(y, axis=-1, keepdims=True)
    yc = y - mu
    var = jnp.mean(yc * yc, axis=-1, keepdims=True)
    return yc * lax.rsqrt(var + LN_EPS) * gain + bias


def _sigmoid(z):
    return 1.0 / (1.0 + jnp.exp(-z))


def _merge_kernel(x_ref, f_ref, r_ref, cb_ref, cc_ref, cx_ref,
                  ccp_ref, cxp_ref, ccn_ref, cxn_ref,
                  gf_ref, gr_ref, gc_ref,
                  wf_ref, wr_ref, wc_ref, wo_ref, cw_ref, ln_ref,
                  o_ref, *, tm, seq):
    i = pl.program_id(0)
    pos0 = (i * tm) % seq
    u = cc_ref[...].astype(F32) * cx_ref[...].astype(F32)
    keep_p = jnp.where(pos0 == 0, 0.0, 1.0).astype(F32)
    keep_n = jnp.where(pos0 + tm == seq, 0.0, 1.0).astype(F32)
    up_row = ccp_ref[7:8, :].astype(F32) * cxp_ref[7:8, :].astype(F32) * keep_p
    un_row = ccn_ref[0:1, :].astype(F32) * cxn_ref[0:1, :].astype(F32) * keep_n
    rows = lax.broadcasted_iota(jnp.int32, u.shape, 0)
    u_prev = jnp.where(rows == 0, up_row, pltpu.roll(u, 1, 0))
    u_next = jnp.where(rows == tm - 1, un_row, pltpu.roll(u, tm - 1, 0))
    y = cw_ref[0:1, :] * u_prev + cw_ref[1:2, :] * u + cw_ref[2:3, :] * u_next
    cin = (cb_ref[...].astype(F32) * y).astype(BF16)

    f = jnp.dot(f_ref[...], wf_ref[...], preferred_element_type=F32)
    merged = _sigmoid(gf_ref[...].astype(F32)) * f
    rr = jnp.dot(r_ref[...], wr_ref[...], preferred_element_type=F32)
    merged = merged + _sigmoid(gr_ref[...].astype(F32)) * rr
    c = jnp.dot(cin, wc_ref[...], preferred_element_type=F32)
    merged = merged + _sigmoid(gc_ref[...].astype(F32)) * c
    mix = jnp.dot(merged.astype(BF16), wo_ref[...], preferred_element_type=F32)
    o_ref[...] = _layer_norm(DEEPNORM_ALPHA * x_ref[...] + mix, ln_ref[0:1, :], ln_ref[1:2, :])


def _merge(x, fmix, rin, rest, w_f, w_r, w_c, w_o, conv_w, ln, layer, seq, tm=512):
    T = x.shape[0]
    nb8 = T // 8
    t8 = tm // 8

    def col(cblk):
        return pl.BlockSpec((tm, D_MODEL), lambda i: (i, cblk))

    def prev(cblk):
        return pl.BlockSpec((8, D_MODEL), lambda i: (jnp.maximum(i * t8 - 1, 0), cblk))

    def nxt(cblk):
        return pl.BlockSpec((8, D_MODEL), lambda i: (jnp.minimum((i + 1) * t8, nb8 - 1), cblk))

    def wspec(rows):
        return pl.BlockSpec((None, rows, D_MODEL), lambda i: (layer, 0, 0),
                            pipeline_mode=pl.Buffered(1))

    return pl.pallas_call(
        functools.partial(_merge_kernel, tm=tm, seq=seq),
        grid=(T // tm,),
        in_specs=[col(0), col(0),
                  pl.BlockSpec((tm, RET_V), lambda i: (i, 0)),
                  col(R_CB), col(R_CC), col(R_CX),
                  prev(R_CC), prev(R_CX), nxt(R_CC), nxt(R_CX),
                  col(R_GF), col(R_GR), col(R_GC),
                  wspec(D_MODEL), wspec(RET_V), wspec(D_MODEL), wspec(D_MODEL),
                  pl.BlockSpec((None, 8, D_MODEL), lambda i: (layer, 0, 0)),
                  pl.BlockSpec((None, 8, D_MODEL), lambda i: (layer, 0, 0))],
        out_specs=pl.BlockSpec((tm, D_MODEL), lambda i: (i, 0)),
        out_shape=jax.ShapeDtypeStruct((T, D_MODEL), F32),
        compiler_params=_cparams(1),
        name="merge",
    )(x, fmix, rin, rest, rest, rest, rest, rest, rest, rest, rest, rest, rest,
      w_f, w_r, w_c, w_o, conv_w, ln)


def _ffn_kernel(x_ref, w1_ref, w2_ref, ln_ref, o_ref, ob_ref):
    tm = x_ref.shape[0]
    ns = max(tm // FFN_SLAB, 1)
    hs = tm // ns
    for sl in range(ns):
        rs = slice(sl * hs, (sl + 1) * hs)
        x = x_ref[rs, :]
        h = jnp.dot(x.astype(BF16), w1_ref[...], preferred_element_type=F32)
        gate = h[:, :FFN_HIDDEN]
        up = h[:, FFN_HIDDEN:]
        a = (gate * _sigmoid(gate) * up).astype(BF16)
        y = jnp.dot(a, w2_ref[...], preferred_element_type=F32)
        out = _layer_norm(DEEPNORM_ALPHA * x + y, ln_ref[0:1, :], ln_ref[1:2, :])
        o_ref[rs, :] = out
        ob_ref[rs, :] = out.astype(BF16)


def _ffn(x, w1, w2, ln, layer, tm=1024):
    T = x.shape[0]
    return pl.pallas_call(
        _ffn_kernel,
        grid=(T // tm,),
        in_specs=[pl.BlockSpec((tm, D_MODEL), lambda i: (i, 0)),
                  pl.BlockSpec((None, D_MODEL, 2 * FFN_HIDDEN), lambda i: (layer, 0, 0),
                               pipeline_mode=pl.Buffered(1)),
                  pl.BlockSpec((None, FFN_HIDDEN, D_MODEL), lambda i: (layer, 0, 0),
                               pipeline_mode=pl.Buffered(1)),
                  pl.BlockSpec((None, 8, D_MODEL), lambda i: (layer, 0, 0))],
        out_specs=[pl.BlockSpec((tm, D_MODEL), lambda i: (i, 0)),
                   pl.BlockSpec((tm, D_MODEL), lambda i: (i, 0))],
        out_shape=[jax.ShapeDtypeStruct((T, D_MODEL), F32),
                   jax.ShapeDtypeStruct((T, D_MODEL), BF16)],
        compiler_params=_cparams(1),
        name="ffn",
    )(x, w1, w2, ln)


def _rope_tables(seq):
    inv_freq = 1.0 / (ROPE_BASE ** jnp.linspace(0.0, 1.0, RET_HEAD_QK // 2, dtype=F32))
    ang = jnp.arange(seq, dtype=F32)[:, None] * inv_freq[None, :]
    return jnp.cos(ang), jnp.sin(ang)


def _pad8(a):
    return jnp.pad(a, ((0, 0), (0, 8 - a.shape[1]), (0, 0)))


def _trunk(x, p):
    batch, seq, _ = x.shape
    T = batch * seq
    cos, sin = _rope_tables(seq)
    ftab = _fourier_tables(seq)
    xf = x.reshape(T, D_MODEL)
    xb = xf.astype(BF16)
    for l in range(DEPTH):
        u = _proj(xb, p["w_fqk"], l, COL_F, D_MODEL, BF16)
        qk = _proj_rope(xb, p["w_fqk"], l, cos, sin, seq)
        rest = _proj(xb, p["w_rest"], l, 0, REST_COLS, BF16, tn=2048)
        fmix = _fourier(u, ftab, batch, seq)
        rin = _retention(qk, rest, p["log_gamma"][l], batch, seq)
        x1 = _merge(xf, fmix, rin, rest, p["w_f"], p["w_r"], p["w_c"], p["w_o"],
                    p["conv_w"], p["ln1"], l, seq)
        xf, xb = _ffn(x1, p["w_ffn_in"], p["w_ffn_out"], p["ln2"], l)
    return xf.reshape(batch, seq, D_MODEL)


def kernel(x_prompt, x_sample, w_in, ret_decay_logit, conv_w, w_fourier_out, w_ret_out,
           w_conv_out, w_o, ln_gain, ln_bias, w_ffn_in, w_ffn_out):
    p = {
        "w_fqk": w_in[:, :, :COL_REST * D_MODEL].astype(BF16),
        "w_rest": w_in[:, :, COL_REST * D_MODEL:].astype(BF16),
        "log_gamma": jax.nn.log_sigmoid(ret_decay_logit.astype(F32)),
        "conv_w": _pad8(conv_w.astype(F32)),
        "w_f": w_fourier_out.astype(BF16),
        "w_r": w_ret_out.astype(BF16),
        "w_c": w_conv_out.astype(BF16),
        "w_o": w_o.astype(BF16),
        "ln1": _pad8(jnp.stack([ln_gain[:, 0], ln_bias[:, 0]], axis=1).astype(F32)),
        "ln2": _pad8(jnp.stack([ln_gain[:, 1], ln_bias[:, 1]], axis=1).astype(F32)),
        "w_ffn_in": w_ffn_in.astype(BF16),
        "w_ffn_out": w_ffn_out.astype(BF16),
    }
    return (_trunk(x_prompt, p), _trunk(x_sample, p))
```
